```python
import jax, jax.numpy as jnp
from jax import lax
import numpy as np

D_MODEL = 4096
BATCH = 4
SEQ = 4096
DEPTH = 2

MIX_WIDTH = D_MODEL
CONV_WIDTH = MIX_WIDTH // 2
CONV_GROUPS = 16
RET_WIDTH = MIX_WIDTH - CONV_WIDTH
RET_HEADS = 8
RET_HEAD_DIM = RET_WIDTH // RET_HEADS
CHUNK = 128
SHORT_CONV = 3
D_FF = 256 * ((8 * D_MODEL // 3 + 255) // 256)
ROPE_BASE = 10000.0
EPS = 1e-6
IN_COLS = 3 * CONV_WIDTH + 4 * RET_WIDTH

kernel_name = "hybrid_shortconv_retention_convffn_encoder"


def rmsnorm(x, g):
    xf = x.astype(jnp.float32)
    y = xf * lax.rsqrt(jnp.mean(xf * xf, axis=-1, keepdims=True) + EPS)
    return (y * g.astype(jnp.float32)).astype(x.dtype)


def dwconv3(x, w, b):
    xp = jnp.pad(x, ((0, 0), (1, 1), (0, 0)))
    return xp[:, :-2] * w[0] + xp[:, 1:-1] * w[1] + xp[:, 2:] * w[2] + b


def rotary(t, cos, sin):
    t1, t2 = jnp.split(t, 2, axis=-1)
    return jnp.concatenate([t1 * cos - t2 * sin, t2 * cos + t1 * sin], axis=-1)


def bidir_retention(q, k, v, log_gamma):
    b, s, h, d = q.shape
    n = s // CHUNK

    def chunks(t):
        return t.reshape(b, n, CHUNK, h, d).transpose(0, 3, 1, 2, 4)

    q, k, v = chunks(q), chunks(k), chunks(v)
    lf = log_gamma[0][:, None]
    lb = log_gamma[1][:, None]
    pos = jnp.arange(CHUNK, dtype=jnp.float32)
    rel = pos[:, None] - pos[None, :]
    dec = jnp.where(rel >= 0,
                    jnp.exp(lf[:, :, None] * jnp.maximum(rel, 0.0)),
                    jnp.exp(lb[:, :, None] * jnp.maximum(-rel, 0.0)))
    scores = jnp.einsum('bhnid,bhnjd->bhnij', q, k) * dec[None, :, None]
    o = jnp.einsum('bhnij,bhnje->bhnie', scores, v)
    wk_f = jnp.exp(lf * (CHUNK - 1 - pos))
    wk_b = jnp.exp(lb * pos)
    kv_f = jnp.einsum('bhncd,bhnce->bhnde', k * wk_f[None, :, None, :, None], v)
    kv_b = jnp.einsum('bhncd,bhnce->bhnde', k * wk_b[None, :, None, :, None], v)
    g_chunk_f = jnp.exp(lf * CHUNK)[None, :, :, None]
    g_chunk_b = jnp.exp(lb * CHUNK)[None, :, :, None]

    def step_f(state, kv):
        return state * g_chunk_f + kv, state

    def step_b(state, kv):
        return state * g_chunk_b + kv, state

    zeros = jnp.zeros(kv_f.shape[:2] + kv_f.shape[3:], kv_f.dtype)
    _, state_f = lax.scan(step_f, zeros, jnp.moveaxis(kv_f, 2, 0))
    _, state_b = lax.scan(step_b, zeros, jnp.moveaxis(kv_b, 2, 0), reverse=True)
    state_f = jnp.moveaxis(state_f, 0, 2)
    state_b = jnp.moveaxis(state_b, 0, 2)
    wq_f = jnp.exp(lf * (pos + 1.0))[None, :, None, :, None]
    wq_b = jnp.exp(lb * (CHUNK - pos))[None, :, None, :, None]
    o = o + wq_f * jnp.einsum('bhnid,bhnde->bhnie', q, state_f) \
          + wq_b * jnp.einsum('bhnid,bhnde->bhnie', q, state_b)
    return o.transpose(0, 2, 3, 1, 4).reshape(b, s, h, d)


def setup_inputs(seed: int = 0) -> dict:
    key = jax.random.key(seed)
    ks = jax.random.split(key, 18)
    f32 = jnp.float32
    nrm = lambda k, shape, scale: jax.random.normal(k, shape, f32) * scale
    base_logit = jnp.asarray(np.log(2.0 ** (5 + np.arange(RET_HEADS)) - 1.0), f32)
    return {
        "x": nrm(ks[0], (BATCH, SEQ, D_MODEL), 1.0),
        "positions": jnp.broadcast_to(jnp.arange(SEQ, dtype=jnp.int32), (BATCH, SEQ)),
        "norm1_g": 1.0 + nrm(ks[1], (DEPTH, D_MODEL), 0.02),
        "w_in": nrm(ks[2], (DEPTH, D_MODEL, IN_COLS), D_MODEL ** -0.5),
        "conv_a_w": nrm(ks[3], (DEPTH, SHORT_CONV, CONV_WIDTH), SHORT_CONV ** -0.5),
        "conv_a_b": nrm(ks[4], (DEPTH, CONV_WIDTH), 0.02),
        "beta_a": 1.0 + nrm(ks[5], (DEPTH, CONV_WIDTH), 0.02),
        "ret_decay_logit": base_logit + nrm(ks[6], (DEPTH, 2, RET_HEADS), 0.05),
        "ret_gn_g": 1.0 + nrm(ks[7], (DEPTH, RET_WIDTH), 0.02),
        "w_out": nrm(ks[8], (DEPTH, MIX_WIDTH, D_MODEL), MIX_WIDTH ** -0.5),
        "norm2_g": 1.0 + nrm(ks[9], (DEPTH, D_MODEL), 0.02),
        "w_gate": nrm(ks[10], (DEPTH, D_MODEL, D_FF), D_MODEL ** -0.5),
        "w_up": nrm(ks[11], (DEPTH, D_MODEL, D_FF), D_MODEL ** -0.5),
        "ffn_conv_w": nrm(ks[12], (DEPTH, SHORT_CONV, D_FF), SHORT_CONV ** -0.5),
        "ffn_conv_b": nrm(ks[13], (DEPTH, D_FF), 0.02),
        "w_down": nrm(ks[14], (DEPTH, D_FF, D_MODEL), D_FF ** -0.5),
        "norm_f_g": 1.0 + nrm(ks[15], (D_MODEL,), 0.02),
    }


def reference(x, positions, norm1_g, w_in, conv_a_w, conv_a_b, beta_a, ret_decay_logit,
              ret_gn_g, w_out, norm2_g, w_gate, w_up, ffn_conv_w, ffn_conv_b, w_down,
              norm_f_g):
    b, s, _ = x.shape
    half = RET_HEAD_DIM // 2
    inv_freq = ROPE_BASE ** (-jnp.arange(half, dtype=jnp.float32) / half)
    ang = positions.astype(jnp.float32)[..., None] * inv_freq
    cos = jnp.cos(ang)[:, :, None, :].astype(x.dtype)
    sin = jnp.sin(ang)[:, :, None, :].astype(x.dtype)
    split_pts = [CONV_WIDTH, 2 * CONV_WIDTH, 3 * CONV_WIDTH,
                 3 * CONV_WIDTH + RET_WIDTH, 3 * CONV_WIDTH + 2 * RET_WIDTH,
                 3 * CONV_WIDTH + 3 * RET_WIDTH]

    for l in range(DEPTH):
        h = rmsnorm(x, norm1_g[l])
        proj = h @ w_in[l]
        a_b, a_c, a_x, r_q, r_k, r_v, r_g = jnp.split(proj, split_pts, axis=-1)
        ya = a_b * dwconv3(a_c * a_x, conv_a_w[l], conv_a_b[l])
        ya = rmsnorm(ya.reshape(b, s, CONV_GROUPS, CONV_WIDTH // CONV_GROUPS),
                     jnp.ones((), jnp.float32)).reshape(b, s, CONV_WIDTH) * beta_a[l]
        q = rotary(r_q.reshape(b, s, RET_HEADS, RET_HEAD_DIM), cos, sin) * (RET_HEAD_DIM ** -0.5)
        k = rotary(r_k.reshape(b, s, RET_HEADS, RET_HEAD_DIM), cos, sin)
        v = r_v.reshape(b, s, RET_HEADS, RET_HEAD_DIM)
        log_gamma = jax.nn.log_sigmoid(ret_decay_logit[l].astype(jnp.float32))
        o = bidir_retention(q, k, v, log_gamma).astype(jnp.float32)
        mu = jnp.mean(o, axis=-1, keepdims=True)
        var = jnp.mean(jnp.square(o - mu), axis=-1, keepdims=True)
        o = ((o - mu) * lax.rsqrt(var + EPS)).reshape(b, s, RET_WIDTH) * ret_gn_g[l]
        yr = o.astype(x.dtype) * jax.nn.silu(r_g)
        y = jnp.concatenate([ya.astype(x.dtype), yr], axis=-1) @ w_out[l]
        x = x + y
        h = rmsnorm(x, norm2_g[l])
        gate = dwconv3(h @ w_gate[l], ffn_conv_w[l], ffn_conv_b[l])
        x = x + (jax.nn.silu(gate) * (h @ w_up[l])) @ w_down[l]

    return rmsnorm(x, norm_f_g)
```

```python
import functools

import jax
import jax.numpy as jnp
from jax import lax
from jax.experimental import pallas as pl
from jax.experimental.pallas import tpu as pltpu

EPS = 1e-6
ROPE_BASE = 10000.0
HEAD_DIM = 256
CONV_GROUP = 128
CHUNK = 128
LANE = 128
BF16_SUBLANES = 16
V7X_VMEM_LIMIT = 56 * 1024 * 1024
FF_TILE = 1024

_BF16 = jnp.bfloat16
_F32 = jnp.float32


def _params(*sem):
    return pltpu.CompilerParams(dimension_semantics=sem, vmem_limit_bytes=V7X_VMEM_LIMIT)


def _dot(a, b):
    return jnp.dot(a, b, preferred_element_type=_F32)


def _fit(dim, tile):
    return tile if dim % tile == 0 else dim


def _rmsnorm_body(x_ref, g_ref, o_ref):
    x = x_ref[...]
    ms = jnp.mean(x * x, axis=-1, keepdims=True)
    o_ref[...] = (x * lax.rsqrt(ms + EPS) * g_ref[...]).astype(o_ref.dtype)


def _rmsnorm(x, g, out_dtype, tm=256):
    m, d = x.shape
    tm = _fit(m, tm)
    return pl.pallas_call(
        _rmsnorm_body,
        grid=(m // tm,),
        in_specs=[pl.BlockSpec((tm, d), lambda i: (i, 0)),
                  pl.BlockSpec((1, d), lambda i: (0, 0))],
        out_specs=pl.BlockSpec((tm, d), lambda i: (i, 0)),
        out_shape=jax.ShapeDtypeStruct((m, d), out_dtype),
        compiler_params=_params("parallel"),
        name="rmsnorm",
    )(x, g.reshape(1, d))


def _mm_body(a_ref, w_ref, o_ref):
    o_ref[...] = _dot(a_ref[...], w_ref[...]).astype(o_ref.dtype)


def _matmul(a, w, out_dtype, tm=1024, tn=1024):
    m, k = a.shape
    n = w.shape[1]
    tm, tn = _fit(m, tm), _fit(n, tn)
    return pl.pallas_call(
        _mm_body,
        grid=(m // tm, n // tn),
        in_specs=[pl.BlockSpec((tm, k), lambda i, j: (i, 0)),
                  pl.BlockSpec((k, tn), lambda i, j: (0, j))],
        out_specs=pl.BlockSpec((tm, tn), lambda i, j: (i, j)),
        out_shape=jax.ShapeDtypeStruct((m, n), out_dtype),
        compiler_params=_params("parallel", "parallel"),
        name="matmul",
    )(a, w)


def _out_proj_body(ya_ref, yr_ref, wa_ref, wr_ref, x_ref, o_ref):
    y = _dot(ya_ref[...], wa_ref[...]) + _dot(yr_ref[...], wr_ref[...])
    o_ref[...] = x_ref[...] + y


def _out_proj(ya, yr, w_out, x, tm=1024, tn=1024):
    m, ka = ya.shape
    kr = yr.shape[1]
    n = w_out.shape[1]
    assert ka == kr
    tm, tn = _fit(m, tm), _fit(n, tn)
    return pl.pallas_call(
        _out_proj_body,
        grid=(m // tm, n // tn),
        in_specs=[pl.BlockSpec((tm, ka), lambda i, j: (i, 0)),
                  pl.BlockSpec((tm, kr), lambda i, j: (i, 0)),
                  pl.BlockSpec((ka, tn), lambda i, j: (0, j)),
                  pl.BlockSpec((kr, tn), lambda i, j: (1, j)),
                  pl.BlockSpec((tm, tn), lambda i, j: (i, j))],
        out_specs=pl.BlockSpec((tm, tn), lambda i, j: (i, j)),
        out_shape=jax.ShapeDtypeStruct((m, n), _F32),
        compiler_params=_params("parallel", "parallel"),
        name="out_proj",
    )(ya, yr, w_out, w_out, x)


def _down_proj_body(a_ref, w_ref, x_ref, o_ref):
    kk = pl.program_id(2)

    @pl.when(kk == 0)
    def _():
        o_ref[...] = x_ref[...]

    o_ref[...] += _dot(a_ref[...], w_ref[...])


def _down_proj(a, w, x, tm=1024, tn=1024, k_steps=4):
    m, k = a.shape
    n = w.shape[1]
    tk = k // k_steps
    assert tk * k_steps == k and tk % LANE == 0
    tm, tn = _fit(m, tm), _fit(n, tn)
    return pl.pallas_call(
        _down_proj_body,
        grid=(m // tm, n // tn, k_steps),
        in_specs=[pl.BlockSpec((tm, tk), lambda i, j, l: (i, l)),
                  pl.BlockSpec((tk, tn), lambda i, j, l: (l, j)),
                  pl.BlockSpec((tm, tn), lambda i, j, l: (i, j))],
        out_specs=pl.BlockSpec((tm, tn), lambda i, j, l: (i, j)),
        out_shape=jax.ShapeDtypeStruct((m, n), _F32),
        compiler_params=_params("parallel", "parallel", "arbitrary"),
        name="down_proj",
    )(a, w, x)


def _neighbour_rows(x, prev_row, next_row):
    tm = x.shape[0]
    rows = lax.broadcasted_iota(jnp.int32, x.shape, 0)
    up = jnp.where(rows == 0, prev_row, pltpu.roll(x, 1, 0))
    dn = jnp.where(rows == tm - 1, next_row, pltpu.roll(x, tm - 1, 0))
    return up, dn


def _halo_specs(tm, tc, col_block, rows_total, halo=BF16_SUBLANES):
    per = tm // halo
    last = rows_total // halo - 1
    prev = pl.BlockSpec((halo, tc), lambda i, c: (jnp.maximum(i * per - 1, 0), col_block(c)))
    nxt = pl.BlockSpec((halo, tc), lambda i, c: (jnp.minimum((i + 1) * per, last), col_block(c)))
    return prev, nxt


def _edge_flags(tiles_per_seq):
    t = pl.program_id(0) % tiles_per_seq
    return (t != 0).astype(_F32), (t != tiles_per_seq - 1).astype(_F32)


def _conv_mixer_body(tiles_per_seq, b_ref, c_ref, x_ref, cp_ref, xp_ref, cn_ref, xn_ref,
                     w_ref, bias_ref, beta_ref, o_ref):
    f = lambda r: r[...].astype(_F32)
    u = f(c_ref) * f(x_ref)
    keep_prev, keep_next = _edge_flags(tiles_per_seq)
    halo = cp_ref.shape[0]
    u_prev = (f(cp_ref) * f(xp_ref))[halo - 1:halo, :] * keep_prev
    u_next = (f(cn_ref) * f(xn_ref))[0:1, :] * keep_next
    up, dn = _neighbour_rows(u, u_prev, u_next)
    w = w_ref[...]
    conv = up * w[0:1, :] + u * w[1:2, :] + dn * w[2:3, :] + bias_ref[...]
    ya = f(b_ref) * conv
    tm, tc = ya.shape
    beta = beta_ref[...]
    for g in range(tc // CONV_GROUP):
        sl = slice(g * CONV_GROUP, (g + 1) * CONV_GROUP)
        yg = ya[:, sl]
        ms = jnp.mean(yg * yg, axis=-1, keepdims=True)
        o_ref[:, sl] = (yg * lax.rsqrt(ms + EPS) * beta[:, sl]).astype(o_ref.dtype)


def _conv_mixer(proj, conv_w, conv_b, beta, seq, tm=512, tc=512):
    m = proj.shape[0]
    cw = conv_w.shape[1]
    tm, tc = _fit(seq, tm), _fit(cw, tc)
    nb = cw // tc
    col = lambda off: (lambda c: c + off * nb)
    main = lambda off: pl.BlockSpec((tm, tc), lambda i, c: (i, c + off * nb))
    cp, cn = _halo_specs(tm, tc, col(1), m)
    xp, xn = _halo_specs(tm, tc, col(2), m)
    vec = lambda rows: pl.BlockSpec((rows, tc), lambda i, c: (0, c))
    return pl.pallas_call(
        functools.partial(_conv_mixer_body, seq // tm),
        grid=(m // tm, nb),
        in_specs=[main(0), main(1), main(2), cp, xp, cn, xn, vec(3), vec(1), vec(1)],
        out_specs=pl.BlockSpec((tm, tc), lambda i, c: (i, c)),
        out_shape=jax.ShapeDtypeStruct((m, cw), _BF16),
        compiler_params=_params("parallel", "parallel"),
        name="conv_mixer",
    )(proj, proj, proj, proj, proj, proj, proj, conv_w, conv_b.reshape(1, cw), beta.reshape(1, cw))


def _rope_tables_body(pos_ref, inv_ref, cos_ref, sin_ref):
    ang = pos_ref[...].astype(_F32) * inv_ref[...]
    cos_ref[...] = jnp.cos(ang)
    sin_ref[...] = jnp.sin(ang)


def _rope_tables(positions, tm=1024):
    m = positions.size
    tm = _fit(m, tm)
    half = HEAD_DIM // 2
    inv_freq = ROPE_BASE ** (-jnp.arange(half, dtype=_F32) / half)
    out = jax.ShapeDtypeStruct((m, half), _F32)
    return pl.pallas_call(
        _rope_tables_body,
        grid=(m // tm,),
        in_specs=[pl.BlockSpec((tm, 1), lambda i: (i, 0)),
                  pl.BlockSpec((1, half), lambda i: (0, 0))],
        out_specs=[pl.BlockSpec((tm, half), lambda i: (i, 0))] * 2,
        out_shape=[out, out],
        compiler_params=_params("parallel"),
        name="rope_tables",
    )(positions.reshape(m, 1), inv_freq.reshape(1, half))


def _rotary(t, cos, sin):
    half = t.shape[-1] // 2
    t1, t2 = t[:, :half], t[:, half:]
    return jnp.concatenate([t1 * cos - t2 * sin, t2 * cos + t1 * sin], axis=-1)


def _retention_body(n_heads, logit_ref, q_ref, k_ref, v_ref, g_ref, cos_ref, sin_ref, gn_ref,
                    o_ref, acc_ref, state_ref):
    h = pl.program_id(1)
    seq, d = q_ref.shape
    n_chunks = seq // CHUNK
    c = CHUNK

    def log_gamma(shape, direction):
        return jax.nn.log_sigmoid(jnp.full(shape, logit_ref[direction * n_heads + h], _F32))

    pos_cd = lax.broadcasted_iota(jnp.int32, (c, d), 0).astype(_F32)
    row = lax.broadcasted_iota(jnp.int32, (c, c), 0)
    col = lax.broadcasted_iota(jnp.int32, (c, c), 1)
    rel = (row - col).astype(_F32)
    dec = jnp.where(rel >= 0,
                    jnp.exp(log_gamma((c, c), 0) * jnp.maximum(rel, 0.0)),
                    jnp.exp(log_gamma((c, c), 1) * jnp.maximum(-rel, 0.0)))
    scale = HEAD_DIM ** -0.5

    def load_qk(n):
        rows = pl.ds(pl.multiple_of(n * c, c), c)
        cos, sin = cos_ref[rows, :], sin_ref[rows, :]
        q = _rotary(q_ref[rows, :].astype(_F32), cos, sin) * scale
        k = _rotary(k_ref[rows, :].astype(_F32), cos, sin)
        return rows, q.astype(_BF16), k

    def kv_summary(k, weight, v):
        return lax.dot_general((k * weight).astype(_BF16), v, (((0,), (0,)), ((), ())),
                               preferred_element_type=_F32)

    lf = log_gamma((c, d), 0)
    wk_f = jnp.exp(lf * (c - 1.0 - pos_cd))
    wq_f = jnp.exp(lf * (pos_cd + 1.0))
    g_chunk_f = jnp.exp(log_gamma((d, d), 0) * float(c))
    state_ref[...] = jnp.zeros_like(state_ref)

    def fwd(n, carry):
        rows, q, k = load_qk(n)
        v = v_ref[rows, :]
        scores = lax.dot_general(q, k.astype(_BF16), (((1,), (1,)), ((), ())),
                                 preferred_element_type=_F32) * dec
        o = _dot(scores.astype(_BF16), v)
        state = state_ref[...]
        o = o + wq_f * _dot(q, state.astype(_BF16))
        acc_ref[rows, :] = o
        state_ref[...] = state * g_chunk_f + kv_summary(k, wk_f, v)
        return carry

    lax.fori_loop(0, n_chunks, fwd, 0)

    lb = log_gamma((c, d), 1)
    wk_b = jnp.exp(lb * pos_cd)
    wq_b = jnp.exp(lb * (c - pos_cd))
    g_chunk_b = jnp.exp(log_gamma((d, d), 1) * float(c))
    state_ref[...] = jnp.zeros_like(state_ref)
    gain = gn_ref[...]

    def bwd(t, carry):
        n = n_chunks - 1 - t
        rows, q, k = load_qk(n)
        v = v_ref[rows, :]
        state = state_ref[...]
        o = acc_ref[rows, :] + wq_b * _dot(q, state.astype(_BF16))
        state_ref[...] = state * g_chunk_b + kv_summary(k, wk_b, v)
        mu = jnp.mean(o, axis=-1, keepdims=True)
        cen = o - mu
        var = jnp.mean(cen * cen, axis=-1, keepdims=True)
        normed = cen * lax.rsqrt(var + EPS) * gain
        gate = g_ref[rows, :].astype(_F32)
        o_ref[rows, :] = (normed * jax.nn.silu(gate)).astype(o_ref.dtype)
        return carry

    lax.fori_loop(0, n_chunks, bwd, 0)


def _retention(proj, cos, sin, decay_logit, gn_g, batch, seq, col0):
    m = proj.shape[0]
    rw = gn_g.shape[0]
    n_heads = rw // HEAD_DIM
    first = col0 // HEAD_DIM
    part = lambda p: pl.BlockSpec((seq, HEAD_DIM), lambda b, h: (b, first + p * n_heads + h))
    table = pl.BlockSpec((seq, HEAD_DIM // 2), lambda b, h: (b, 0))
    return pl.pallas_call(
        functools.partial(_retention_body, n_heads),
        grid=(batch, n_heads),
        in_specs=[pl.BlockSpec(memory_space=pltpu.SMEM),
                  part(0), part(1), part(2), part(3), table, table,
                  pl.BlockSpec((1, HEAD_DIM), lambda b, h: (0, h))],
        out_specs=pl.BlockSpec((seq, HEAD_DIM), lambda b, h: (b, h)),
        out_shape=jax.ShapeDtypeStruct((m, rw), _BF16),
        scratch_shapes=[pltpu.VMEM((seq, HEAD_DIM), _F32),
                        pltpu.VMEM((HEAD_DIM, HEAD_DIM), _F32)],
        compiler_params=_params("parallel", "parallel"),
        name="retention",
    )(decay_logit.reshape(-1), proj, proj, proj, proj, cos, sin, gn_g.reshape(1, rw))


def _ffn_gate_body(tiles_per_seq, g_ref, gp_ref, gn_ref, u_ref, w_ref, bias_ref, o_ref):
    f = lambda r: r[...].astype(_F32)
    g = f(g_ref)
    keep_prev, keep_next = _edge_flags(tiles_per_seq)
    halo = gp_ref.shape[0]
    up, dn = _neighbour_rows(g, f(gp_ref)[halo - 1:halo, :] * keep_prev, f(gn_ref)[0:1, :] * keep_next)
    w = w_ref[...]
    gate = up * w[0:1, :] + g * w[1:2, :] + dn * w[2:3, :] + bias_ref[...]
    o_ref[...] = (jax.nn.silu(gate) * f(u_ref)).astype(o_ref.dtype)


def _ffn_gate(gate_pre, up, conv_w, conv_b, seq, tm=512, tc=1024):
    m, ff = gate_pre.shape
    tm, tc = _fit(seq, tm), _fit(ff, tc)
    main = pl.BlockSpec((tm, tc), lambda i, c: (i, c))
    prev, nxt = _halo_specs(tm, tc, lambda c: c, m)
    vec = lambda rows: pl.BlockSpec((rows, tc), lambda i, c: (0, c))
    return pl.pallas_call(
        functools.partial(_ffn_gate_body, seq // tm),
        grid=(m // tm, ff // tc),
        in_specs=[main, prev, nxt, main, vec(3), vec(1)],
        out_specs=main,
        out_shape=jax.ShapeDtypeStruct((m, ff), _BF16),
        compiler_params=_params("parallel", "parallel"),
        name="ffn_gate",
    )(gate_pre, gate_pre, gate_pre, up, conv_w, conv_b.reshape(1, ff))


def _pad_cols(w, n):
    return jnp.pad(w, [(0, 0)] * (w.ndim - 1) + [(0, n - w.shape[-1])])


def kernel(x, positions, norm1_g, w_in, conv_a_w, conv_a_b, beta_a, ret_decay_logit, ret_gn_g, w_out,
           norm2_g, w_gate, w_up, ffn_conv_w, ffn_conv_b, w_down, norm_f_g):
    batch, seq, d_model = x.shape
    depth = w_in.shape[0]
    conv_width = conv_a_w.shape[-1]
    d_ff = w_gate.shape[-1]
    ff_pad = -(-d_ff // FF_TILE) * FF_TILE
    m = batch * seq

    xf = x.reshape(m, d_model)
    cos, sin = _rope_tables(positions)

    for l in range(depth):
        h = _rmsnorm(xf, norm1_g[l], _BF16)
        proj = _matmul(h, w_in[l].astype(_BF16), _BF16)
        ya = _conv_mixer(proj, conv_a_w[l], conv_a_b[l], beta_a[l], seq)
        yr = _retention(proj, cos, sin, ret_decay_logit[l], ret_gn_g[l], batch, seq, 3 * conv_width)
        xf = _out_proj(ya, yr, w_out[l].astype(_BF16), xf)
        h = _rmsnorm(xf, norm2_g[l], _BF16)
        gate_pre = _matmul(h, _pad_cols(w_gate[l].astype(_BF16), ff_pad), _BF16)
        up = _matmul(h, _pad_cols(w_up[l].astype(_BF16), ff_pad), _BF16)
        hidden = _ffn_gate(gate_pre, up, _pad_cols(ffn_conv_w[l], ff_pad), _pad_cols(ffn_conv_b[l], ff_pad), seq)
        w_dn = jnp.pad(w_down[l].astype(_BF16), ((0, ff_pad - d_ff), (0, 0)))
        xf = _down_proj(hidden, w_dn, xf)

    return _rmsnorm(xf, norm_f_g, x.dtype).reshape(batch, seq, d_model)
```

```python
import functools

import jax
import jax.numpy as jnp
from jax import lax
from jax.experimental import pallas as pl
from jax.experimental.pallas import tpu as pltpu

EPS = 1e-6
ROPE_BASE = 10000.0
HEAD_DIM = 256
CONV_GROUP = 128
CHUNK = 256
BF16_SUBLANES = 16
V7X_VMEM_LIMIT = 56 * 1024 * 1024

_BF16 = jnp.bfloat16
_F32 = jnp.float32


def _params(*sem):
    return pltpu.CompilerParams(dimension_semantics=sem, vmem_limit_bytes=V7X_VMEM_LIMIT)


def _dot(a, b):
    return jnp.dot(a, b, preferred_element_type=_F32)


def _fit(dim, tile):
    return tile if dim % tile == 0 else dim


def _rmsnorm_body(x_ref, g_ref, o_ref):
    x = x_ref[...]
    ms = jnp.mean(x * x, axis=-1, keepdims=True)
    o_ref[...] = (x * lax.rsqrt(ms + EPS) * g_ref[...]).astype(o_ref.dtype)


def _rmsnorm(x, g, out_dtype, tm=256):
    m, d = x.shape
    tm = _fit(m, tm)
    return pl.pallas_call(
        _rmsnorm_body,
        grid=(m // tm,),
        in_specs=[pl.BlockSpec((tm, d), lambda i: (i, 0)),
                  pl.BlockSpec((1, d), lambda i: (0, 0))],
        out_specs=pl.BlockSpec((tm, d), lambda i: (i, 0)),
        out_shape=jax.ShapeDtypeStruct((m, d), out_dtype),
        compiler_params=_params("parallel"),
        name="rmsnorm",
    )(x, g.reshape(1, d))


def _in_proj_body(a_ref, w_ref, o_ref):
    o_ref[...] = _dot(a_ref[...], w_ref[...].astype(_BF16)).astype(o_ref.dtype)


def _in_proj(a, w, layer, tm=1024, tn=512):
    m, k = a.shape
    n = w.shape[2]
    tm, tn = _fit(m, tm), _fit(n, tn)
    return pl.pallas_call(
        _in_proj_body,
        grid=(m // tm, n // tn),
        in_specs=[pl.BlockSpec((tm, k), lambda i, j: (i, 0)),
                  pl.BlockSpec((None, k, tn), lambda i, j: (layer, 0, j))],
        out_specs=pl.BlockSpec((tm, tn), lambda i, j: (i, j)),
        out_shape=jax.ShapeDtypeStruct((m, n), _BF16),
        compiler_params=_params("parallel", "parallel"),
        name="in_proj",
    )(a, w)


def _out_proj_body(ya_ref, yr_ref, wa_ref, wr_ref, x_ref, o_ref):
    y = _dot(ya_ref[...], wa_ref[...].astype(_BF16)) + _dot(yr_ref[...], wr_ref[...].astype(_BF16))
    o_ref[...] = x_ref[...] + y


def _out_proj(ya, yr, w_out, layer, x, tm=1024, tn=512):
    m, ka = ya.shape
    kr = yr.shape[1]
    n = w_out.shape[2]
    assert ka == kr
    tm, tn = _fit(m, tm), _fit(n, tn)
    return pl.pallas_call(
        _out_proj_body,
        grid=(m // tm, n // tn),
        in_specs=[pl.BlockSpec((tm, ka), lambda i, j: (i, 0)),
                  pl.BlockSpec((tm, kr), lambda i, j: (i, 0)),
                  pl.BlockSpec((None, ka, tn), lambda i, j: (layer, 0, j)),
                  pl.BlockSpec((None, kr, tn), lambda i, j: (layer, 1, j)),
                  pl.BlockSpec((tm, tn), lambda i, j: (i, j))],
        out_specs=pl.BlockSpec((tm, tn), lambda i, j: (i, j)),
        out_shape=jax.ShapeDtypeStruct((m, n), _F32),
        compiler_params=_params("parallel", "parallel"),
        name="out_proj",
    )(ya, yr, w_out, w_out, x)


def _cast_body(w_ref, o_ref):
    o_ref[...] = w_ref[...].astype(o_ref.dtype)


def _cast_layer(w, layer, tr=256):
    _, r, c = w.shape
    tr = _fit(r, tr)
    return pl.pallas_call(
        _cast_body,
        grid=(r // tr,),
        in_specs=[pl.BlockSpec((None, tr, c), lambda i: (layer, i, 0))],
        out_specs=pl.BlockSpec((tr, c), lambda i: (i, 0)),
        out_shape=jax.ShapeDtypeStruct((r, c), _BF16),
        compiler_params=_params("parallel"),
        name="cast_weight",
    )(w)


def _down_proj_body(a_ref, w_ref, x_ref, o_ref):
    o_ref[...] = x_ref[...] + _dot(a_ref[...], w_ref[...])


def _down_proj(a, w, x, tm=512, tn=512):
    m, k = a.shape
    n = w.shape[1]
    tm, tn = _fit(m, tm), _fit(n, tn)
    return pl.pallas_call(
        _down_proj_body,
        grid=(m // tm, n // tn),
        in_specs=[pl.BlockSpec((tm, k), lambda i, j: (i, 0)),
                  pl.BlockSpec((k, tn), lambda i, j: (0, j)),
                  pl.BlockSpec((tm, tn), lambda i, j: (i, j))],
        out_specs=pl.BlockSpec((tm, tn), lambda i, j: (i, j)),
        out_shape=jax.ShapeDtypeStruct((m, n), _F32),
        compiler_params=_params("parallel", "parallel"),
        name="down_proj",
    )(a, w, x)


def _neighbour_rows(x, prev_row, next_row):
    tm = x.shape[0]
    rows = lax.broadcasted_iota(jnp.int32, x.shape, 0)
    up = jnp.where(rows == 0, prev_row, pltpu.roll(x, 1, 0))
    dn = jnp.where(rows == tm - 1, next_row, pltpu.roll(x, tm - 1, 0))
    return up, dn


def _halo_specs(tm, tc, col_block, rows_total, halo=BF16_SUBLANES):
    per = tm // halo
    last = rows_total // halo - 1
    prev = pl.BlockSpec((halo, tc), lambda i, c: (jnp.maximum(i * per - 1, 0), col_block(c)))
    nxt = pl.BlockSpec((halo, tc), lambda i, c: (jnp.minimum((i + 1) * per, last), col_block(c)))
    return prev, nxt


def _conv_mixer_body(tiles_per_seq, b_ref, c_ref, x_ref, cp_ref, xp_ref, cn_ref, xn_ref,
                     w_ref, bias_ref, beta_ref, o_ref):
    f = lambda r: r[...].astype(_F32)
    u = f(c_ref) * f(x_ref)
    t = pl.program_id(0) % tiles_per_seq
    keep_prev = (t != 0).astype(_F32)
    keep_next = (t != tiles_per_seq - 1).astype(_F32)
    halo = cp_ref.shape[0]
    u_prev = (f(cp_ref) * f(xp_ref))[halo - 1:halo, :] * keep_prev
    u_next = (f(cn_ref) * f(xn_ref))[0:1, :] * keep_next
    up, dn = _neighbour_rows(u, u_prev, u_next)
    w = w_ref[...]
    conv = up * w[0:1, :] + u * w[1:2, :] + dn * w[2:3, :] + bias_ref[...]
    ya = f(b_ref) * conv
    tm, tc = ya.shape
    beta = beta_ref[...]
    for g in range(tc // CONV_GROUP):
        sl = slice(g * CONV_GROUP, (g + 1) * CONV_GROUP)
        yg = ya[:, sl]
        ms = jnp.mean(yg * yg, axis=-1, keepdims=True)
        o_ref[:, sl] = (yg * lax.rsqrt(ms + EPS) * beta[:, sl]).astype(o_ref.dtype)


def _conv_mixer(proj, conv_w, conv_b, beta, layer, seq, tm=512, tc=512):
    m = proj.shape[0]
    cw = conv_w.shape[-1]
    tm, tc = _fit(seq, tm), _fit(cw, tc)
    nb = cw // tc
    col = lambda off: (lambda c: c + off * nb)
    main = lambda off: pl.BlockSpec((tm, tc), lambda i, c: (i, c + off * nb))
    cp, cn = _halo_specs(tm, tc, col(1), m)
    xp, xn = _halo_specs(tm, tc, col(2), m)
    vec = lambda rows: pl.BlockSpec((None, rows, tc), lambda i, c: (layer, 0, c))
    depth = conv_w.shape[0]
    return pl.pallas_call(
        functools.partial(_conv_mixer_body, seq // tm),
        grid=(m // tm, nb),
        in_specs=[main(0), main(1), main(2), cp, xp, cn, xn, vec(3), vec(1), vec(1)],
        out_specs=pl.BlockSpec((tm, tc), lambda i, c: (i, c)),
        out_shape=jax.ShapeDtypeStruct((m, cw), _BF16),
        compiler_params=_params("parallel", "parallel"),
        name="conv_mixer",
    )(proj, proj, proj, proj, proj, proj, proj, conv_w, conv_b.reshape(depth, 1, cw), beta.reshape(depth, 1, cw))


def _rope_tables_body(pos_ref, inv_ref, cos_ref, sin_ref):
    ang = pos_ref[...].astype(_F32) * inv_ref[...]
    cos_ref[...] = jnp.cos(ang)
    sin_ref[...] = jnp.sin(ang)


def _rope_tables(positions, tm=1024):
    m = positions.size
    tm = _fit(m, tm)
    half = HEAD_DIM // 2
    inv_freq = ROPE_BASE ** (-jnp.arange(half, dtype=_F32) / half)
    out = jax.ShapeDtypeStruct((m, half), _F32)
    return pl.pallas_call(
        _rope_tables_body,
        grid=(m // tm,),
        in_specs=[pl.BlockSpec((tm, 1), lambda i: (i, 0)),
                  pl.BlockSpec((1, half), lambda i: (0, 0))],
        out_specs=[pl.BlockSpec((tm, half), lambda i: (i, 0))] * 2,
        out_shape=[out, out],
        compiler_params=_params("parallel"),
        name="rope_tables",
    )(positions.reshape(m, 1), inv_freq.reshape(1, half))


def _rotary(t, cos, sin):
    half = t.shape[-1] // 2
    t1, t2 = t[:, :half], t[:, half:]
    return jnp.concatenate([t1 * cos - t2 * sin, t2 * cos + t1 * sin], axis=-1)


def _retention_body(n_heads, logit_ref, q_ref, k_ref, v_ref, g_ref, cos_ref, sin_ref, gn_ref,
                    o_ref, krot_ref, kvf_ref, kvb_ref, sf_ref, sb_ref, state_ref):
    h = pl.program_id(1)
    seq, d = q_ref.shape
    c = CHUNK
    n_chunks = seq // c

    def log_gamma(shape, direction):
        return jax.nn.log_sigmoid(jnp.full(shape, logit_ref[direction * n_heads + h], _F32))

    def chunk_rows(n):
        return pl.ds(pl.multiple_of(n * c, c), c)

    pos_cd = lax.broadcasted_iota(jnp.int32, (c, d), 0).astype(_F32)
    lf = log_gamma((c, d), 0)
    lb = log_gamma((c, d), 1)

    wk_f = jnp.exp(lf * (c - 1.0 - pos_cd))
    wk_b = jnp.exp(lb * pos_cd)

    def summarise(n, carry):
        rows = chunk_rows(n)
        k = _rotary(k_ref[rows, :].astype(_F32), cos_ref[rows, :], sin_ref[rows, :])
        krot_ref[rows, :] = k.astype(_BF16)
        v = v_ref[rows, :]
        contract_rows = (((0,), (0,)), ((), ()))
        kvf_ref[n] = lax.dot_general((k * wk_f).astype(_BF16), v, contract_rows, preferred_element_type=_F32)
        kvb_ref[n] = lax.dot_general((k * wk_b).astype(_BF16), v, contract_rows, preferred_element_type=_F32)
        return carry

    lax.fori_loop(0, n_chunks, summarise, 0, unroll=2)

    def scan(summary_ref, out_ref, log_g, order):
        g_chunk = jnp.exp(log_g * float(c))
        state_ref[...] = jnp.zeros_like(state_ref)

        def step(t, carry):
            n = order(t)
            state = state_ref[...]
            out_ref[n] = state.astype(_BF16)
            state_ref[...] = state * g_chunk + summary_ref[n]
            return carry

        lax.fori_loop(0, n_chunks, step, 0)

    scan(kvf_ref, sf_ref, log_gamma((1, d), 0), lambda t: t)
    scan(kvb_ref, sb_ref, log_gamma((1, d), 1), lambda t: n_chunks - 1 - t)

    row = lax.broadcasted_iota(jnp.int32, (c, c), 0)
    col = lax.broadcasted_iota(jnp.int32, (c, c), 1)
    rel = (row - col).astype(_F32)
    dec = jnp.where(rel >= 0,
                    jnp.exp(log_gamma((c, c), 0) * jnp.maximum(rel, 0.0)),
                    jnp.exp(log_gamma((c, c), 1) * jnp.maximum(-rel, 0.0)))
    wq_f = jnp.exp(lf * (pos_cd + 1.0))
    wq_b = jnp.exp(lb * (c - pos_cd))
    gain = gn_ref[...]
    scale = HEAD_DIM ** -0.5

    def output(n, carry):
        rows = chunk_rows(n)
        q = (_rotary(q_ref[rows, :].astype(_F32), cos_ref[rows, :], sin_ref[rows, :]) * scale).astype(_BF16)
        scores = lax.dot_general(q, krot_ref[rows, :], (((1,), (1,)), ((), ())),
                                 preferred_element_type=_F32) * dec
        o = _dot(scores.astype(_BF16), v_ref[rows, :])
        o = o + wq_f * _dot(q, sf_ref[n]) + wq_b * _dot(q, sb_ref[n])
        mu = jnp.mean(o, axis=-1, keepdims=True)
        cen = o - mu
        var = jnp.mean(cen * cen, axis=-1, keepdims=True)
        normed = cen * lax.rsqrt(var + EPS) * gain
        gate = g_ref[rows, :].astype(_F32)
        o_ref[rows, :] = (normed * jax.nn.silu(gate)).astype(o_ref.dtype)
        return carry

    lax.fori_loop(0, n_chunks, output, 0, unroll=2)


def _retention(proj, cos, sin, decay_logit, gn_g, layer, batch, seq, col0):
    m = proj.shape[0]
    depth, rw = gn_g.shape
    n_heads = rw // HEAD_DIM
    first = col0 // HEAD_DIM
    n_chunks = seq // CHUNK
    part = lambda p: pl.BlockSpec((seq, HEAD_DIM), lambda b, h: (b, first + p * n_heads + h))
    table = pl.BlockSpec((seq, HEAD_DIM // 2), lambda b, h: (b, 0))
    summaries = pltpu.VMEM((n_chunks, HEAD_DIM, HEAD_DIM), _F32)
    states = pltpu.VMEM((n_chunks, HEAD_DIM, HEAD_DIM), _BF16)
    return pl.pallas_call(
        functools.partial(_retention_body, n_heads),
        grid=(batch, n_heads),
        in_specs=[pl.BlockSpec(memory_space=pltpu.SMEM),
                  part(0), part(1), part(2), part(3), table, table,
                  pl.BlockSpec((None, 1, HEAD_DIM), lambda b, h: (layer, 0, h))],
        out_specs=pl.BlockSpec((seq, HEAD_DIM), lambda b, h: (b, h)),
        out_shape=jax.ShapeDtypeStruct((m, rw), _BF16),
        scratch_shapes=[pltpu.VMEM((seq, HEAD_DIM), _BF16), summaries, summaries, states, states,
                        pltpu.VMEM((HEAD_DIM, HEAD_DIM), _F32)],
        compiler_params=_params("parallel", "parallel"),
        name="retention",
    )(decay_logit[layer].reshape(-1), proj, proj, proj, proj, cos, sin, gn_g.reshape(depth, 1, rw))


def _ffn_up_body(tiles_per_seq, n_sub, h_ref, hp_ref, hn_ref, wg_ref, wu_ref, cw_ref, cb_ref, o_ref, lhs_ref):
    tm = h_ref.shape[0]
    halo = hp_ref.shape[0]

    @pl.when(pl.program_id(1) == 0)
    def _():
        lhs_ref[0:halo, :] = hp_ref[...]
        lhs_ref[halo:halo + tm, :] = h_ref[...]
        lhs_ref[halo + tm:, :] = hn_ref[...]

    wg = wg_ref[...].astype(_BF16)
    wu = wu_ref[...].astype(_BF16)
    w = cw_ref[...]
    bias = cb_ref[...]
    t = pl.program_id(0) % tiles_per_seq
    keep_prev = (t != 0).astype(_F32)
    keep_next = (t != tiles_per_seq - 1).astype(_F32)

    sub = tm // n_sub
    gs = []
    for s in range(n_sub):
        lo = halo + s * sub - (halo if s == 0 else 0)
        hi = halo + (s + 1) * sub + (halo if s == n_sub - 1 else 0)
        g = _dot(lhs_ref[lo:hi, :], wg)
        if s == 0:
            g_prev = g[halo - 1:halo, :] * keep_prev
            g = g[halo:, :]
        if s == n_sub - 1:
            g_next = g[sub:sub + 1, :] * keep_next
            g = g[:sub, :]
        gs.append(g)
    for s in range(n_sub):
        u = _dot(lhs_ref[halo + s * sub:halo + (s + 1) * sub, :], wu)
        prev_row = g_prev if s == 0 else gs[s - 1][sub - 1:sub, :]
        next_row = g_next if s == n_sub - 1 else gs[s + 1][0:1, :]
        g_up, g_dn = _neighbour_rows(gs[s], prev_row, next_row)
        gate = g_up * w[0:1, :] + gs[s] * w[1:2, :] + g_dn * w[2:3, :] + bias
        o_ref[s * sub:(s + 1) * sub, :] = (jax.nn.silu(gate) * u).astype(o_ref.dtype)


def _ffn_up(h, w_gate, w_up, conv_w, conv_b, layer, seq, tm=1024, tn=256, n_sub=4):
    m, k = h.shape
    depth, _, ff = w_gate.shape
    tm, tn = _fit(seq, tm), _fit(ff, tn)
    halo = BF16_SUBLANES
    per = tm // halo
    last = m // halo - 1
    weight = pl.BlockSpec((None, k, tn), lambda i, j: (layer, 0, j))
    vec = lambda rows: pl.BlockSpec((None, rows, tn), lambda i, j: (layer, 0, j))
    return pl.pallas_call(
        functools.partial(_ffn_up_body, seq // tm, n_sub),
        grid=(m // tm, ff // tn),
        in_specs=[pl.BlockSpec((tm, k), lambda i, j: (i, 0)),
                  pl.BlockSpec((halo, k), lambda i, j: (jnp.maximum(i * per - 1, 0), 0)),
                  pl.BlockSpec((halo, k), lambda i, j: (jnp.minimum((i + 1) * per, last), 0)),
                  weight, weight, vec(3), vec(1)],
        out_specs=pl.BlockSpec((tm, tn), lambda i, j: (i, j)),
        out_shape=jax.ShapeDtypeStruct((m, ff), _BF16),
        scratch_shapes=[pltpu.VMEM((tm + 2 * halo, k), _BF16)],
        compiler_params=_params("parallel", "arbitrary"),
        name="ffn_up",
    )(h, h, h, w_gate, w_up, conv_w, conv_b.reshape(depth, 1, ff))


def kernel(x, positions, norm1_g, w_in, conv_a_w, conv_a_b, beta_a, ret_decay_logit, ret_gn_g, w_out,
           norm2_g, w_gate, w_up, ffn_conv_w, ffn_conv_b, w_down, norm_f_g):
    batch, seq, d_model = x.shape
    depth = w_in.shape[0]
    conv_width = conv_a_w.shape[-1]
    m = batch * seq

    xf = x.reshape(m, d_model)
    cos, sin = _rope_tables(positions)

    for l in range(depth):
        h = _rmsnorm(xf, norm1_g[l], _BF16)
        proj = _in_proj(h, w_in, l)
        ya = _conv_mixer(proj, conv_a_w, conv_a_b, beta_a, l, seq)
        yr = _retention(proj, cos, sin, ret_decay_logit, ret_gn_g, l, batch, seq, 3 * conv_width)
        xf = _out_proj(ya, yr, w_out, l, xf)
        h = _rmsnorm(xf, norm2_g[l], _BF16)
        hidden = _ffn_up(h, w_gate, w_up, ffn_conv_w, ffn_conv_b, l, seq)
        xf = _down_proj(hidden, _cast_layer(w_down, l), xf)

    return _rmsnorm(xf, norm_f_g, x.dtype).reshape(batch, seq, d_model)
```

```python
import functools

import jax
import jax.numpy as jnp
from jax import lax
from jax.experimental import pallas as pl
from jax.experimental.pallas import tpu as pltpu

EPS = 1e-6
ROPE_BASE = 10000.0
HEAD_DIM = 256
CONV_GROUP = 128
CHUNK = 256
BF16_SUBLANES = 16
V7X_VMEM_LIMIT = 56 * 1024 * 1024

_BF16 = jnp.bfloat16
_F32 = jnp.float32


def _params(*sem):
    return pltpu.CompilerParams(dimension_semantics=sem, vmem_limit_bytes=V7X_VMEM_LIMIT)


def _dot(a, b):
    return jnp.dot(a, b, preferred_element_type=_F32)


def _fit(dim, tile):
    return tile if dim % tile == 0 else dim


SSQ_LANES = 128


def _row_scale(ssq, width):
    return lax.rsqrt(jnp.sum(ssq, axis=-1, keepdims=True) / width + EPS)


def _emit_norm_inputs(x_new, gain_ref, xg_ref, ssq_ref, col_step):
    xg_ref[...] = (x_new * gain_ref[...]).astype(xg_ref.dtype)
    sq = x_new * x_new
    part = sq[:, 0:SSQ_LANES]
    for c in range(1, sq.shape[1] // SSQ_LANES):
        part = part + sq[:, c * SSQ_LANES:(c + 1) * SSQ_LANES]

    @pl.when(col_step == 0)
    def _():
        ssq_ref[...] = part

    @pl.when(col_step != 0)
    def _():
        ssq_ref[...] += part


def _norm_inputs_out(m, n, tm, tn):
    specs = [pl.BlockSpec((tm, tn), lambda i, j: (i, j)), pl.BlockSpec((tm, SSQ_LANES), lambda i, j: (i, 0))]
    shapes = [jax.ShapeDtypeStruct((m, n), _BF16), jax.ShapeDtypeStruct((m, SSQ_LANES), _F32)]
    return specs, shapes


def _stream_in_body(x_ref, g_ref, xg_ref, ssq_ref):
    _emit_norm_inputs(x_ref[...], g_ref, xg_ref, ssq_ref, pl.program_id(1))


def _stream_in(x, gain, tm=256):
    m, d = x.shape
    tm = _fit(m, tm)
    specs, shapes = _norm_inputs_out(m, d, tm, d)
    return pl.pallas_call(
        _stream_in_body,
        grid=(m // tm, 1),
        in_specs=[pl.BlockSpec((tm, d), lambda i, j: (i, 0)),
                  pl.BlockSpec((1, d), lambda i, j: (0, 0))],
        out_specs=specs,
        out_shape=shapes,
        compiler_params=_params("parallel", "arbitrary"),
        name="stream_in",
    )(x, gain.reshape(1, d))


def _rmsnorm_body(x_ref, g_ref, o_ref):
    x = x_ref[...]
    ms = jnp.mean(x * x, axis=-1, keepdims=True)
    o_ref[...] = (x * lax.rsqrt(ms + EPS) * g_ref[...]).astype(o_ref.dtype)


def _rmsnorm(x, g, out_dtype, tm=256):
    m, d = x.shape
    tm = _fit(m, tm)
    return pl.pallas_call(
        _rmsnorm_body,
        grid=(m // tm,),
        in_specs=[pl.BlockSpec((tm, d), lambda i: (i, 0)),
                  pl.BlockSpec((1, d), lambda i: (0, 0))],
        out_specs=pl.BlockSpec((tm, d), lambda i: (i, 0)),
        out_shape=jax.ShapeDtypeStruct((m, d), out_dtype),
        compiler_params=_params("parallel"),
        name="rmsnorm",
    )(x, g.reshape(1, d))


def _in_proj_body(a_ref, ssq_ref, w_ref, o_ref):
    scale = _row_scale(ssq_ref[...], a_ref.shape[1])
    o_ref[...] = (_dot(a_ref[...], w_ref[...].astype(_BF16)) * scale).astype(o_ref.dtype)


def _in_proj(a, ssq, w, layer, tm=1024, tn=512):
    m, k = a.shape
    n = w.shape[2]
    tm, tn = _fit(m, tm), _fit(n, tn)
    return pl.pallas_call(
        _in_proj_body,
        grid=(m // tm, n // tn),
        in_specs=[pl.BlockSpec((tm, k), lambda i, j: (i, 0)),
                  pl.BlockSpec((tm, SSQ_LANES), lambda i, j: (i, 0)),
                  pl.BlockSpec((None, k, tn), lambda i, j: (layer, 0, j))],
        out_specs=pl.BlockSpec((tm, tn), lambda i, j: (i, j)),
        out_shape=jax.ShapeDtypeStruct((m, n), _BF16),
        compiler_params=_params("parallel", "parallel"),
        name="in_proj",
    )(a, ssq, w)


def _out_proj_body(ya_ref, yr_ref, wa_ref, wr_ref, x_ref, gain_ref, o_ref, xg_ref, ssq_ref):
    y = _dot(ya_ref[...], wa_ref[...].astype(_BF16)) + _dot(yr_ref[...], wr_ref[...].astype(_BF16))
    x_new = x_ref[...] + y
    o_ref[...] = x_new
    _emit_norm_inputs(x_new, gain_ref, xg_ref, ssq_ref, pl.program_id(1))


def _out_proj(ya, yr, w_out, layer, x, gain, tm=1024, tn=512):
    m, ka = ya.shape
    kr = yr.shape[1]
    n = w_out.shape[2]
    assert ka == kr
    tm, tn = _fit(m, tm), _fit(n, tn)
    tile = pl.BlockSpec((tm, tn), lambda i, j: (i, j))
    specs, shapes = _norm_inputs_out(m, n, tm, tn)
    return pl.pallas_call(
        _out_proj_body,
        grid=(m // tm, n // tn),
        in_specs=[pl.BlockSpec((tm, ka), lambda i, j: (i, 0)),
                  pl.BlockSpec((tm, kr), lambda i, j: (i, 0)),
                  pl.BlockSpec((None, ka, tn), lambda i, j: (layer, 0, j)),
                  pl.BlockSpec((None, kr, tn), lambda i, j: (layer, 1, j)),
                  tile,
                  pl.BlockSpec((1, tn), lambda i, j: (0, j))],
        out_specs=[tile] + specs,
        out_shape=[jax.ShapeDtypeStruct((m, n), _F32)] + shapes,
        compiler_params=_params("parallel", "arbitrary"),
        name="out_proj",
    )(ya, yr, w_out, w_out, x, gain.reshape(1, n))


def _cast_body(w_ref, o_ref):
    o_ref[...] = w_ref[...].astype(o_ref.dtype)


def _cast_layer(w, layer, tr=256):
    _, r, c = w.shape
    tr = _fit(r, tr)
    return pl.pallas_call(
        _cast_body,
        grid=(r // tr,),
        in_specs=[pl.BlockSpec((None, tr, c), lambda i: (layer, i, 0))],
        out_specs=pl.BlockSpec((tr, c), lambda i: (i, 0)),
        out_shape=jax.ShapeDtypeStruct((r, c), _BF16),
        compiler_params=_params("parallel"),
        name="cast_weight",
    )(w)


def _down_proj_body(a_ref, w_ref, x_ref, gain_ref, o_ref, xg_ref, ssq_ref):
    x_new = x_ref[...] + _dot(a_ref[...], w_ref[...])
    o_ref[...] = x_new
    _emit_norm_inputs(x_new, gain_ref, xg_ref, ssq_ref, pl.program_id(1))


def _down_proj_last_body(a_ref, w_ref, x_ref, o_ref):
    o_ref[...] = x_ref[...] + _dot(a_ref[...], w_ref[...])


def _down_proj(a, w, x, gain=None, tm=512, tn=512):
    m, k = a.shape
    n = w.shape[1]
    tm, tn = _fit(m, tm), _fit(n, tn)
    tile = pl.BlockSpec((tm, tn), lambda i, j: (i, j))
    in_specs = [pl.BlockSpec((tm, k), lambda i, j: (i, 0)), pl.BlockSpec((k, tn), lambda i, j: (0, j)), tile]
    if gain is None:
        return pl.pallas_call(
            _down_proj_last_body,
            grid=(m // tm, n // tn),
            in_specs=in_specs,
            out_specs=tile,
            out_shape=jax.ShapeDtypeStruct((m, n), _F32),
            compiler_params=_params("parallel", "parallel"),
            name="down_proj_last",
        )(a, w, x)
    specs, shapes = _norm_inputs_out(m, n, tm, tn)
    return pl.pallas_call(
        _down_proj_body,
        grid=(m // tm, n // tn),
        in_specs=in_specs + [pl.BlockSpec((1, tn), lambda i, j: (0, j))],
        out_specs=[tile] + specs,
        out_shape=[jax.ShapeDtypeStruct((m, n), _F32)] + shapes,
        compiler_params=_params("parallel", "arbitrary"),
        name="down_proj",
    )(a, w, x, gain.reshape(1, n))


def _neighbour_rows(x, prev_row, next_row):
    tm = x.shape[0]
    rows = lax.broadcasted_iota(jnp.int32, x.shape, 0)
    up = jnp.where(rows == 0, prev_row, pltpu.roll(x, 1, 0))
    dn = jnp.where(rows == tm - 1, next_row, pltpu.roll(x, tm - 1, 0))
    return up, dn


def _halo_specs(tm, tc, col_block, rows_total, halo=BF16_SUBLANES):
    per = tm // halo
    last = rows_total // halo - 1
    prev = pl.BlockSpec((halo, tc), lambda i, c: (jnp.maximum(i * per - 1, 0), col_block(c)))
    nxt = pl.BlockSpec((halo, tc), lambda i, c: (jnp.minimum((i + 1) * per, last), col_block(c)))
    return prev, nxt


def _conv_mixer_body(tiles_per_seq, b_ref, c_ref, x_ref, cp_ref, xp_ref, cn_ref, xn_ref,
                     w_ref, bias_ref, beta_ref, o_ref):
    f = lambda r: r[...].astype(_F32)
    u = f(c_ref) * f(x_ref)
    t = pl.program_id(0) % tiles_per_seq
    keep_prev = (t != 0).astype(_F32)
    keep_next = (t != tiles_per_seq - 1).astype(_F32)
    halo = cp_ref.shape[0]
    u_prev = (f(cp_ref) * f(xp_ref))[halo - 1:halo, :] * keep_prev
    u_next = (f(cn_ref) * f(xn_ref))[0:1, :] * keep_next
    up, dn = _neighbour_rows(u, u_prev, u_next)
    w = w_ref[...]
    conv = up * w[0:1, :] + u * w[1:2, :] + dn * w[2:3, :] + bias_ref[...]
    ya = f(b_ref) * conv
    tm, tc = ya.shape
    beta = beta_ref[...]
    for g in range(tc // CONV_GROUP):
        sl = slice(g * CONV_GROUP, (g + 1) * CONV_GROUP)
        yg = ya[:, sl]
        ms = jnp.mean(yg * yg, axis=-1, keepdims=True)
        o_ref[:, sl] = (yg * lax.rsqrt(ms + EPS) * beta[:, sl]).astype(o_ref.dtype)


def _conv_mixer(proj, conv_w, conv_b, beta, layer, seq, tm=1024, tc=1024):
    m = proj.shape[0]
    cw = conv_w.shape[-1]
    tm, tc = _fit(seq, tm), _fit(cw, tc)
    nb = cw // tc
    col = lambda off: (lambda c: c + off * nb)
    main = lambda off: pl.BlockSpec((tm, tc), lambda i, c: (i, c + off * nb))
    cp, cn = _halo_specs(tm, tc, col(1), m)
    xp, xn = _halo_specs(tm, tc, col(2), m)
    vec = lambda rows: pl.BlockSpec((None, rows, tc), lambda i, c: (layer, 0, c))
    depth = conv_w.shape[0]
    return pl.pallas_call(
        functools.partial(_conv_mixer_body, seq // tm),
        grid=(m // tm, nb),
        in_specs=[main(0), main(1), main(2), cp, xp, cn, xn, vec(3), vec(1), vec(1)],
        out_specs=pl.BlockSpec((tm, tc), lambda i, c: (i, c)),
        out_shape=jax.ShapeDtypeStruct((m, cw), _BF16),
        compiler_params=_params("parallel", "parallel"),
        name="conv_mixer",
    )(proj, proj, proj, proj, proj, proj, proj, conv_w, conv_b.reshape(depth, 1, cw), beta.reshape(depth, 1, cw))


def _rope_tables_body(pos_ref, inv_ref, cos_ref, sin_ref):
    ang = pos_ref[...].astype(_F32) * inv_ref[...]
    cos_ref[...] = jnp.cos(ang)
    sin_ref[...] = jnp.sin(ang)


def _rope_tables(positions, tm=1024):
    m = positions.size
    tm = _fit(m, tm)
    half = HEAD_DIM // 2
    inv_freq = ROPE_BASE ** (-jnp.arange(half, dtype=_F32) / half)
    out = jax.ShapeDtypeStruct((m, half), _F32)
    return pl.pallas_call(
        _rope_tables_body,
        grid=(m // tm,),
        in_specs=[pl.BlockSpec((tm, 1), lambda i: (i, 0)),
                  pl.BlockSpec((1, half), lambda i: (0, 0))],
        out_specs=[pl.BlockSpec((tm, half), lambda i: (i, 0))] * 2,
        out_shape=[out, out],
        compiler_params=_params("parallel"),
        name="rope_tables",
    )(positions.reshape(m, 1), inv_freq.reshape(1, half))


def _rotary(t, cos, sin):
    half = t.shape[-1] // 2
    t1, t2 = t[:, :half], t[:, half:]
    return jnp.concatenate([t1 * cos - t2 * sin, t2 * cos + t1 * sin], axis=-1)


def _retention_body(n_heads, logit_ref, q_ref, k_ref, v_ref, g_ref, cos_ref, sin_ref, gn_ref,
                    o_ref, krot_ref, kvf_ref, kvb_ref, sf_ref, sb_ref, state_ref):
    h = pl.program_id(1)
    seq, d = q_ref.shape
    c = CHUNK
    n_chunks = seq // c

    def log_gamma(shape, direction):
        return jax.nn.log_sigmoid(jnp.full(shape, logit_ref[direction * n_heads + h], _F32))

    def chunk_rows(n):
        return pl.ds(pl.multiple_of(n * c, c), c)

    pos_cd = lax.broadcasted_iota(jnp.int32, (c, d), 0).astype(_F32)
    lf = log_gamma((c, d), 0)
    lb = log_gamma((c, d), 1)

    wk_f = jnp.exp(lf * (c - 1.0 - pos_cd))
    wk_b = jnp.exp(lb * pos_cd)

    def summarise(n, carry):
        rows = chunk_rows(n)
        k = _rotary(k_ref[rows, :].astype(_F32), cos_ref[rows, :], sin_ref[rows, :])
        krot_ref[rows, :] = k.astype(_BF16)
        v = v_ref[rows, :]
        contract_rows = (((0,), (0,)), ((), ()))
        kvf_ref[n] = lax.dot_general((k * wk_f).astype(_BF16), v, contract_rows, preferred_element_type=_F32)
        kvb_ref[n] = lax.dot_general((k * wk_b).astype(_BF16), v, contract_rows, preferred_element_type=_F32)
        return carry

    lax.fori_loop(0, n_chunks, summarise, 0, unroll=2)

    def scan(summary_ref, out_ref, log_g, order):
        g_chunk = jnp.exp(log_g * float(c))
        state_ref[...] = jnp.zeros_like(state_ref)

        def step(t, carry):
            n = order(t)
            state = state_ref[...]
            out_ref[n] = state.astype(_BF16)
            state_ref[...] = state * g_chunk + summary_ref[n]
            return carry

        lax.fori_loop(0, n_chunks, step, 0)

    scan(kvf_ref, sf_ref, log_gamma((1, d), 0), lambda t: t)
    scan(kvb_ref, sb_ref, log_gamma((1, d), 1), lambda t: n_chunks - 1 - t)

    row = lax.broadcasted_iota(jnp.int32, (c, c), 0)
    col = lax.broadcasted_iota(jnp.int32, (c, c), 1)
    rel = (row - col).astype(_F32)
    dec = jnp.where(rel >= 0,
                    jnp.exp(log_gamma((c, c), 0) * jnp.maximum(rel, 0.0)),
                    jnp.exp(log_gamma((c, c), 1) * jnp.maximum(-rel, 0.0)))
    wq_f = jnp.exp(lf * (pos_cd + 1.0))
    wq_b = jnp.exp(lb * (c - pos_cd))
    gain = gn_ref[...]
    scale = HEAD_DIM ** -0.5

    def output(n, carry):
        rows = chunk_rows(n)
        q = (_rotary(q_ref[rows, :].astype(_F32), cos_ref[rows, :], sin_ref[rows, :]) * scale).astype(_BF16)
        scores = lax.dot_general(q, krot_ref[rows, :], (((1,), (1,)), ((), ())),
                                 preferred_element_type=_F32) * dec
        o = _dot(scores.astype(_BF16), v_ref[rows, :])
        o = o + wq_f * _dot(q, sf_ref[n]) + wq_b * _dot(q, sb_ref[n])
        mu = jnp.mean(o, axis=-1, keepdims=True)
        cen = o - mu
        var = jnp.mean(cen * cen, axis=-1, keepdims=True)
        normed = cen * lax.rsqrt(var + EPS) * gain
        gate = g_ref[rows, :].astype(_F32)
        o_ref[rows, :] = (normed * jax.nn.silu(gate)).astype(o_ref.dtype)
        return carry

    lax.fori_loop(0, n_chunks, output, 0, unroll=2)


def _retention(proj, cos, sin, decay_logit, gn_g, layer, batch, seq, col0):
    m = proj.shape[0]
    depth, rw = gn_g.shape
    n_heads = rw // HEAD_DIM
    first = col0 // HEAD_DIM
    n_chunks = seq // CHUNK
    part = lambda p: pl.BlockSpec((seq, HEAD_DIM), lambda b, h: (b, first + p * n_heads + h))
    table = pl.BlockSpec((seq, HEAD_DIM // 2), lambda b, h: (b, 0))
    summaries = pltpu.VMEM((n_chunks, HEAD_DIM, HEAD_DIM), _F32)
    states = pltpu.VMEM((n_chunks, HEAD_DIM, HEAD_DIM), _BF16)
    return pl.pallas_call(
        functools.partial(_retention_body, n_heads),
        grid=(batch, n_heads),
        in_specs=[pl.BlockSpec(memory_space=pltpu.SMEM),
                  part(0), part(1), part(2), part(3), table, table,
                  pl.BlockSpec((None, 1, HEAD_DIM), lambda b, h: (layer, 0, h))],
        out_specs=pl.BlockSpec((seq, HEAD_DIM), lambda b, h: (b, h)),
        out_shape=jax.ShapeDtypeStruct((m, rw), _BF16),
        scratch_shapes=[pltpu.VMEM((seq, HEAD_DIM), _BF16), summaries, summaries, states, states,
                        pltpu.VMEM((HEAD_DIM, HEAD_DIM), _F32)],
        compiler_params=_params("parallel", "parallel"),
        name="retention",
    )(decay_logit[layer].reshape(-1), proj, proj, proj, proj, cos, sin, gn_g.reshape(depth, 1, rw))


def _ffn_up_body(tiles_per_seq, n_sub, h_ref, hp_ref, hn_ref, ssq_ref, ssqp_ref, ssqn_ref,
                 wg_ref, wu_ref, cw_ref, cb_ref, o_ref):
    tm, d_model = h_ref.shape
    halo = hp_ref.shape[0]
    wg = wg_ref[...].astype(_BF16)
    wu = wu_ref[...].astype(_BF16)
    w = cw_ref[...]
    bias = cb_ref[...]
    t = pl.program_id(0) % tiles_per_seq
    scale_prev = _row_scale(ssqp_ref[...], d_model)[-1:, :] * (t != 0).astype(_F32)
    scale_next = _row_scale(ssqn_ref[...], d_model)[0:1, :] * (t != tiles_per_seq - 1).astype(_F32)

    sub = tm // n_sub
    rows = lambda s: slice(s * sub, (s + 1) * sub)
    scale = lambda s: _row_scale(ssq_ref[rows(s), :], d_model)

    def gate_pre(s):
        parts = ([hp_ref[...]] if s == 0 else []) + [h_ref[rows(s), :]] + ([hn_ref[...]] if s == n_sub - 1 else [])
        g = _dot(parts[0] if len(parts) == 1 else jnp.concatenate(parts, axis=0), wg)
        lo = halo if s == 0 else 0
        before = g[lo - 1:lo, :] * scale_prev if s == 0 else None
        after = g[lo + sub:lo + sub + 1, :] * scale_next if s == n_sub - 1 else None
        return g[lo:lo + sub, :] * scale(s), before, after

    g_cur, prev_row, g_next = gate_pre(0)
    for s in range(n_sub):
        if s + 1 < n_sub:
            g_after, _, g_next = gate_pre(s + 1)
            next_row = g_after[0:1, :]
        else:
            next_row = g_next
        u = _dot(h_ref[rows(s), :], wu) * scale(s)
        g_up, g_dn = _neighbour_rows(g_cur, prev_row, next_row)
        gate = g_up * w[0:1, :] + g_cur * w[1:2, :] + g_dn * w[2:3, :] + bias
        o_ref[rows(s), :] = (jax.nn.silu(gate) * u).astype(o_ref.dtype)
        prev_row = g_cur[sub - 1:sub, :]
        if s + 1 < n_sub:
            g_cur = g_after


def _ffn_up(h, ssq, w_gate, w_up, conv_w, conv_b, layer, seq, tm=2048, tn=256, sub=512):
    m, k = h.shape
    depth, _, ff = w_gate.shape
    tm, tn = _fit(seq, tm), _fit(ff, tn)
    halo = BF16_SUBLANES
    f32_rows = 8
    once = dict(pipeline_mode=pl.Buffered(1))
    before = lambda rows: (lambda i, j: (jnp.maximum(i * (tm // rows) - 1, 0), 0))
    after = lambda rows: (lambda i, j: (jnp.minimum((i + 1) * (tm // rows), m // rows - 1), 0))
    weight = pl.BlockSpec((None, k, tn), lambda i, j: (layer, 0, j))
    vec = lambda rows: pl.BlockSpec((None, rows, tn), lambda i, j: (layer, 0, j))
    return pl.pallas_call(
        functools.partial(_ffn_up_body, seq // tm, tm // _fit(tm, sub)),
        grid=(m // tm, ff // tn),
        in_specs=[pl.BlockSpec((tm, k), lambda i, j: (i, 0), **once),
                  pl.BlockSpec((halo, k), before(halo)),
                  pl.BlockSpec((halo, k), after(halo)),
                  pl.BlockSpec((tm, SSQ_LANES), lambda i, j: (i, 0), **once),
                  pl.BlockSpec((f32_rows, SSQ_LANES), before(f32_rows)),
                  pl.BlockSpec((f32_rows, SSQ_LANES), after(f32_rows)),
                  weight, weight, vec(3), vec(1)],
        out_specs=pl.BlockSpec((tm, tn), lambda i, j: (i, j)),
        out_shape=jax.ShapeDtypeStruct((m, ff), _BF16),
        compiler_params=_params("parallel", "parallel"),
        name="ffn_up",
    )(h, h, h, ssq, ssq, ssq, w_gate, w_up, conv_w, conv_b.reshape(depth, 1, ff))


def kernel(x, positions, norm1_g, w_in, conv_a_w, conv_a_b, beta_a, ret_decay_logit, ret_gn_g, w_out,
           norm2_g, w_gate, w_up, ffn_conv_w, ffn_conv_b, w_down, norm_f_g):
    batch, seq, d_model = x.shape
    depth = w_in.shape[0]
    conv_width = conv_a_w.shape[-1]
    m = batch * seq

    xf = x.reshape(m, d_model)
    cos, sin = _rope_tables(positions)
    xg, ssq = _stream_in(xf, norm1_g[0])

    for l in range(depth):
        proj = _in_proj(xg, ssq, w_in, l)
        ya = _conv_mixer(proj, conv_a_w, conv_a_b, beta_a, l, seq)
        yr = _retention(proj, cos, sin, ret_decay_logit, ret_gn_g, l, batch, seq, 3 * conv_width)
        xf, xg, ssq = _out_proj(ya, yr, w_out, l, xf, norm2_g[l])
        hidden = _ffn_up(xg, ssq, w_gate, w_up, ffn_conv_w, ffn_conv_b, l, seq)
        w_dn = _cast_layer(w_down, l)
        if l + 1 < depth:
            xf, xg, ssq = _down_proj(hidden, w_dn, xf, norm1_g[l + 1])
        else:
            xf = _down_proj(hidden, w_dn, xf)

    return _rmsnorm(xf, norm_f_g, x.dtype).reshape(batch, seq, d_model)
```

```python
import functools

import jax
import jax.numpy as jnp
from jax import lax
from jax.experimental import pallas as pl
from jax.experimental.pallas import tpu as pltpu

EPS = 1e-6
ROPE_BASE = 10000.0
HEAD_DIM = 256
CONV_GROUP = 128
CHUNK = 256
BF16_SUBLANES = 16
V7X_VMEM_LIMIT = 56 * 1024 * 1024

_BF16 = jnp.bfloat16
_F32 = jnp.float32


def _params(*sem):
    return pltpu.CompilerParams(dimension_semantics=sem, vmem_limit_bytes=V7X_VMEM_LIMIT)


def _dot(a, b):
    return jnp.dot(a, b, preferred_element_type=_F32)


def _fit(dim, tile):
    return tile if dim % tile == 0 else dim


SSQ_LANES = 128


def _row_scale(ssq, width):
    return lax.rsqrt(jnp.sum(ssq, axis=-1, keepdims=True) / width + EPS)


def _emit_norm_inputs(x_new, gain_ref, xg_ref, ssq_ref, col_step):
    xg_ref[...] = (x_new * gain_ref[...]).astype(xg_ref.dtype)
    sq = x_new * x_new
    part = sq[:, 0:SSQ_LANES]
    for c in range(1, sq.shape[1] // SSQ_LANES):
        part = part + sq[:, c * SSQ_LANES:(c + 1) * SSQ_LANES]

    @pl.when(col_step == 0)
    def _():
        ssq_ref[...] = part

    @pl.when(col_step != 0)
    def _():
        ssq_ref[...] += part


def _norm_inputs_out(m, n, tm, tn):
    specs = [pl.BlockSpec((tm, tn), lambda i, j: (i, j)), pl.BlockSpec((tm, SSQ_LANES), lambda i, j: (i, 0))]
    shapes = [jax.ShapeDtypeStruct((m, n), _BF16), jax.ShapeDtypeStruct((m, SSQ_LANES), _F32)]
    return specs, shapes


def _stream_in_body(x_ref, g_ref, xg_ref, ssq_ref):
    _emit_norm_inputs(x_ref[...], g_ref, xg_ref, ssq_ref, pl.program_id(1))


def _stream_in(x, gain, tm=256):
    m, d = x.shape
    tm = _fit(m, tm)
    specs, shapes = _norm_inputs_out(m, d, tm, d)
    return pl.pallas_call(
        _stream_in_body,
        grid=(m // tm, 1),
        in_specs=[pl.BlockSpec((tm, d), lambda i, j: (i, 0)),
                  pl.BlockSpec((1, d), lambda i, j: (0, 0))],
        out_specs=specs,
        out_shape=shapes,
        compiler_params=_params("parallel", "arbitrary"),
        name="stream_in",
    )(x, gain.reshape(1, d))


def _rmsnorm_body(x_ref, g_ref, o_ref):
    x = x_ref[...]
    ms = jnp.mean(x * x, axis=-1, keepdims=True)
    o_ref[...] = (x * lax.rsqrt(ms + EPS) * g_ref[...]).astype(o_ref.dtype)


def _rmsnorm(x, g, out_dtype, tm=256):
    m, d = x.shape
    tm = _fit(m, tm)
    return pl.pallas_call(
        _rmsnorm_body,
        grid=(m // tm,),
        in_specs=[pl.BlockSpec((tm, d), lambda i: (i, 0)),
                  pl.BlockSpec((1, d), lambda i: (0, 0))],
        out_specs=pl.BlockSpec((tm, d), lambda i: (i, 0)),
        out_shape=jax.ShapeDtypeStruct((m, d), out_dtype),
        compiler_params=_params("parallel"),
        name="rmsnorm",
    )(x, g.reshape(1, d))


def _in_proj_body(a_ref, ssq_ref, w_ref, o_ref):
    scale = _row_scale(ssq_ref[...], a_ref.shape[1])
    o_ref[...] = (_dot(a_ref[...], w_ref[...].astype(_BF16)) * scale).astype(o_ref.dtype)


def _in_proj(a, ssq, w, layer, tm=1024, tn=512):
    m, k = a.shape
    n = w.shape[2]
    tm, tn = _fit(m, tm), _fit(n, tn)
    return pl.pallas_call(
        _in_proj_body,
        grid=(m // tm, n // tn),
        in_specs=[pl.BlockSpec((tm, k), lambda i, j: (i, 0)),
                  pl.BlockSpec((tm, SSQ_LANES), lambda i, j: (i, 0)),
                  pl.BlockSpec((None, k, tn), lambda i, j: (layer, 0, j))],
        out_specs=pl.BlockSpec((tm, tn), lambda i, j: (i, j)),
        out_shape=jax.ShapeDtypeStruct((m, n), _BF16),
        compiler_params=_params("parallel", "parallel"),
        name="in_proj",
    )(a, ssq, w)


def _out_proj_body(ya_ref, yr_ref, wa_ref, wr_ref, x_ref, gain_ref, o_ref, xg_ref, ssq_ref):
    y = _dot(ya_ref[...], wa_ref[...].astype(_BF16)) + _dot(yr_ref[...], wr_ref[...].astype(_BF16))
    x_new = x_ref[...] + y
    o_ref[...] = x_new
    _emit_norm_inputs(x_new, gain_ref, xg_ref, ssq_ref, pl.program_id(1))


def _out_proj(ya, yr, w_out, layer, x, gain, tm=1024, tn=512):
    m, ka = ya.shape
    kr = yr.shape[1]
    n = w_out.shape[2]
    assert ka == kr
    tm, tn = _fit(m, tm), _fit(n, tn)
    tile = pl.BlockSpec((tm, tn), lambda i, j: (i, j))
    specs, shapes = _norm_inputs_out(m, n, tm, tn)
    return pl.pallas_call(
        _out_proj_body,
        grid=(m // tm, n // tn),
        in_specs=[pl.BlockSpec((tm, ka), lambda i, j: (i, 0)),
                  pl.BlockSpec((tm, kr), lambda i, j: (i, 0)),
                  pl.BlockSpec((None, ka, tn), lambda i, j: (layer, 0, j)),
                  pl.BlockSpec((None, kr, tn), lambda i, j: (layer, 1, j)),
                  tile,
                  pl.BlockSpec((1, tn), lambda i, j: (0, j))],
        out_specs=[tile] + specs,
        out_shape=[jax.ShapeDtypeStruct((m, n), _F32)] + shapes,
        compiler_params=_params("parallel", "arbitrary"),
        name="out_proj",
    )(ya, yr, w_out, w_out, x, gain.reshape(1, n))


def _down_proj_body(a_ref, w_ref, x_ref, gain_ref, o_ref, xg_ref, ssq_ref):
    x_new = x_ref[...] + _dot(a_ref[...], w_ref[...])
    o_ref[...] = x_new
    _emit_norm_inputs(x_new, gain_ref, xg_ref, ssq_ref, pl.program_id(1))


def _down_proj_last_body(a_ref, w_ref, x_ref, o_ref):
    o_ref[...] = x_ref[...] + _dot(a_ref[...], w_ref[...])


def _down_proj(a, w, x, gain=None, tm=512, tn=512):
    m, k = a.shape
    n = w.shape[1]
    tm, tn = _fit(m, tm), _fit(n, tn)
    tile = pl.BlockSpec((tm, tn), lambda i, j: (i, j))
    in_specs = [pl.BlockSpec((tm, k), lambda i, j: (i, 0)), pl.BlockSpec((k, tn), lambda i, j: (0, j)), tile]
    if gain is None:
        return pl.pallas_call(
            _down_proj_last_body,
            grid=(m // tm, n // tn),
            in_specs=in_specs,
            out_specs=tile,
            out_shape=jax.ShapeDtypeStruct((m, n), _F32),
            compiler_params=_params("parallel", "parallel"),
            name="down_proj_last",
        )(a, w, x)
    specs, shapes = _norm_inputs_out(m, n, tm, tn)
    return pl.pallas_call(
        _down_proj_body,
        grid=(m // tm, n // tn),
        in_specs=in_specs + [pl.BlockSpec((1, tn), lambda i, j: (0, j))],
        out_specs=[tile] + specs,
        out_shape=[jax.ShapeDtypeStruct((m, n), _F32)] + shapes,
        compiler_params=_params("parallel", "arbitrary"),
        name="down_proj",
    )(a, w, x, gain.reshape(1, n))


def _neighbour_rows(x, prev_row, next_row):
    tm = x.shape[0]
    rows = lax.broadcasted_iota(jnp.int32, x.shape, 0)
    up = jnp.where(rows == 0, prev_row, pltpu.roll(x, 1, 0))
    dn = jnp.where(rows == tm - 1, next_row, pltpu.roll(x, tm - 1, 0))
    return up, dn


def _halo_specs(tm, tc, col_block, rows_total, halo=BF16_SUBLANES):
    per = tm // halo
    last = rows_total // halo - 1
    prev = pl.BlockSpec((halo, tc), lambda i, c: (jnp.maximum(i * per - 1, 0), col_block(c)))
    nxt = pl.BlockSpec((halo, tc), lambda i, c: (jnp.minimum((i + 1) * per, last), col_block(c)))
    return prev, nxt


def _conv_mixer_body(tiles_per_seq, b_ref, c_ref, x_ref, cp_ref, xp_ref, cn_ref, xn_ref,
                     w_ref, bias_ref, beta_ref, o_ref):
    f = lambda r: r[...].astype(_F32)
    u = f(c_ref) * f(x_ref)
    t = pl.program_id(0) % tiles_per_seq
    keep_prev = (t != 0).astype(_F32)
    keep_next = (t != tiles_per_seq - 1).astype(_F32)
    halo = cp_ref.shape[0]
    u_prev = (f(cp_ref) * f(xp_ref))[halo - 1:halo, :] * keep_prev
    u_next = (f(cn_ref) * f(xn_ref))[0:1, :] * keep_next
    up, dn = _neighbour_rows(u, u_prev, u_next)
    w = w_ref[...]
    conv = up * w[0:1, :] + u * w[1:2, :] + dn * w[2:3, :] + bias_ref[...]
    ya = f(b_ref) * conv
    tm, tc = ya.shape
    beta = beta_ref[...]
    for g in range(tc // CONV_GROUP):
        sl = slice(g * CONV_GROUP, (g + 1) * CONV_GROUP)
        yg = ya[:, sl]
        ms = jnp.mean(yg * yg, axis=-1, keepdims=True)
        o_ref[:, sl] = (yg * lax.rsqrt(ms + EPS) * beta[:, sl]).astype(o_ref.dtype)


def _conv_mixer(proj, conv_w, conv_b, beta, layer, seq, tm=1024, tc=1024):
    m = proj.shape[0]
    cw = conv_w.shape[-1]
    tm, tc = _fit(seq, tm), _fit(cw, tc)
    nb = cw // tc
    col = lambda off: (lambda c: c + off * nb)
    main = lambda off: pl.BlockSpec((tm, tc), lambda i, c: (i, c + off * nb))
    cp, cn = _halo_specs(tm, tc, col(1), m)
    xp, xn = _halo_specs(tm, tc, col(2), m)
    vec = lambda rows: pl.BlockSpec((None, rows, tc), lambda i, c: (layer, 0, c))
    depth = conv_w.shape[0]
    return pl.pallas_call(
        functools.partial(_conv_mixer_body, seq // tm),
        grid=(m // tm, nb),
        in_specs=[main(0), main(1), main(2), cp, xp, cn, xn, vec(3), vec(1), vec(1)],
        out_specs=pl.BlockSpec((tm, tc), lambda i, c: (i, c)),
        out_shape=jax.ShapeDtypeStruct((m, cw), _BF16),
        compiler_params=_params("parallel", "parallel"),
        name="conv_mixer",
    )(proj, proj, proj, proj, proj, proj, proj, conv_w, conv_b.reshape(depth, 1, cw), beta.reshape(depth, 1, cw))


def _rope_tables_body(pos_ref, inv_ref, cos_ref, sin_ref):
    ang = pos_ref[...].astype(_F32) * inv_ref[...]
    cos_ref[...] = jnp.cos(ang)
    sin_ref[...] = jnp.sin(ang)


def _rope_tables(positions, tm=1024):
    m = positions.size
    tm = _fit(m, tm)
    half = HEAD_DIM // 2
    inv_freq = ROPE_BASE ** (-jnp.arange(half, dtype=_F32) / half)
    out = jax.ShapeDtypeStruct((m, half), _F32)
    return pl.pallas_call(
        _rope_tables_body,
        grid=(m // tm,),
        in_specs=[pl.BlockSpec((tm, 1), lambda i: (i, 0)),
                  pl.BlockSpec((1, half), lambda i: (0, 0))],
        out_specs=[pl.BlockSpec((tm, half), lambda i: (i, 0))] * 2,
        out_shape=[out, out],
        compiler_params=_params("parallel"),
        name="rope_tables",
    )(positions.reshape(m, 1), inv_freq.reshape(1, half))


def _rotary(t, cos, sin):
    half = t.shape[-1] // 2
    t1, t2 = t[:, :half], t[:, half:]
    return jnp.concatenate([t1 * cos - t2 * sin, t2 * cos + t1 * sin], axis=-1)


def _retention_body(n_heads, logit_ref, q_ref, k_ref, v_ref, g_ref, cos_ref, sin_ref, gn_ref,
                    o_ref, krot_ref, kvf_ref, kvb_ref, sf_ref, sb_ref, state_ref):
    h = pl.program_id(1)
    seq, d = q_ref.shape
    c = CHUNK
    n_chunks = seq // c

    def log_gamma(shape, direction):
        return jax.nn.log_sigmoid(jnp.full(shape, logit_ref[direction * n_heads + h], _F32))

    def chunk_rows(n):
        return pl.ds(pl.multiple_of(n * c, c), c)

    pos_cd = lax.broadcasted_iota(jnp.int32, (c, d), 0).astype(_F32)
    lf = log_gamma((c, d), 0)
    lb = log_gamma((c, d), 1)

    wk_f = jnp.exp(lf * (c - 1.0 - pos_cd))
    wk_b = jnp.exp(lb * pos_cd)

    def summarise(n, carry):
        rows = chunk_rows(n)
        k = _rotary(k_ref[rows, :].astype(_F32), cos_ref[rows, :], sin_ref[rows, :])
        krot_ref[rows, :] = k.astype(_BF16)
        v = v_ref[rows, :]
        contract_rows = (((0,), (0,)), ((), ()))
        kvf_ref[n] = lax.dot_general((k * wk_f).astype(_BF16), v, contract_rows, preferred_element_type=_F32)
        kvb_ref[n] = lax.dot_general((k * wk_b).astype(_BF16), v, contract_rows, preferred_element_type=_F32)
        return carry

    lax.fori_loop(0, n_chunks, summarise, 0, unroll=4)

    def scan(summary_ref, out_ref, log_g, order):
        g_chunk = jnp.exp(log_g * float(c))
        state_ref[...] = jnp.zeros_like(state_ref)

        def step(t, carry):
            n = order(t)
            state = state_ref[...]
            out_ref[n] = state.astype(_BF16)
            state_ref[...] = state * g_chunk + summary_ref[n]
            return carry

        lax.fori_loop(0, n_chunks, step, 0, unroll=4)

    scan(kvf_ref, sf_ref, log_gamma((1, d), 0), lambda t: t)
    scan(kvb_ref, sb_ref, log_gamma((1, d), 1), lambda t: n_chunks - 1 - t)

    row = lax.broadcasted_iota(jnp.int32, (c, c), 0)
    col = lax.broadcasted_iota(jnp.int32, (c, c), 1)
    rel = (row - col).astype(_F32)
    scale = HEAD_DIM ** -0.5
    dec = jnp.where(rel >= 0,
                    jnp.exp(log_gamma((c, c), 0) * jnp.maximum(rel, 0.0)),
                    jnp.exp(log_gamma((c, c), 1) * jnp.maximum(-rel, 0.0))) * scale
    wq_f = jnp.exp(lf * (pos_cd + 1.0)) * scale
    wq_b = jnp.exp(lb * (c - pos_cd)) * scale
    gain = gn_ref[...]

    def output(n, carry):
        rows = chunk_rows(n)
        q = _rotary(q_ref[rows, :].astype(_F32), cos_ref[rows, :], sin_ref[rows, :]).astype(_BF16)
        scores = lax.dot_general(q, krot_ref[rows, :], (((1,), (1,)), ((), ())),
                                 preferred_element_type=_F32) * dec
        o = _dot(scores.astype(_BF16), v_ref[rows, :])
        o = o + wq_f * _dot(q, sf_ref[n]) + wq_b * _dot(q, sb_ref[n])
        mu = jnp.mean(o, axis=-1, keepdims=True)
        cen = o - mu
        var = jnp.mean(cen * cen, axis=-1, keepdims=True)
        normed = cen * lax.rsqrt(var + EPS) * gain
        gate = g_ref[rows, :].astype(_F32)
        o_ref[rows, :] = (normed * jax.nn.silu(gate)).astype(o_ref.dtype)
        return carry

    lax.fori_loop(0, n_chunks, output, 0, unroll=4)


def _retention(proj, cos, sin, decay_logit, gn_g, layer, batch, seq, col0):
    m = proj.shape[0]
    depth, rw = gn_g.shape
    n_heads = rw // HEAD_DIM
    first = col0 // HEAD_DIM
    n_chunks = seq // CHUNK
    part = lambda p: pl.BlockSpec((seq, HEAD_DIM), lambda b, h: (b, first + p * n_heads + h))
    table = pl.BlockSpec((seq, HEAD_DIM // 2), lambda b, h: (b, 0), pipeline_mode=pl.Buffered(1))
    summaries = pltpu.VMEM((n_chunks, HEAD_DIM, HEAD_DIM), _F32)
    states = pltpu.VMEM((n_chunks, HEAD_DIM, HEAD_DIM), _BF16)
    return pl.pallas_call(
        functools.partial(_retention_body, n_heads),
        grid=(batch, n_heads),
        in_specs=[pl.BlockSpec(memory_space=pltpu.SMEM),
                  part(0), part(1), part(2), part(3), table, table,
                  pl.BlockSpec((None, 1, HEAD_DIM), lambda b, h: (layer, 0, h))],
        out_specs=pl.BlockSpec((seq, HEAD_DIM), lambda b, h: (b, h)),
        out_shape=jax.ShapeDtypeStruct((m, rw), _BF16),
        scratch_shapes=[pltpu.VMEM((seq, HEAD_DIM), _BF16), summaries, summaries, states, states,
                        pltpu.VMEM((HEAD_DIM, HEAD_DIM), _F32)],
        compiler_params=_params("parallel", "parallel"),
        name="retention",
    )(decay_logit[layer].reshape(-1), proj, proj, proj, proj, cos, sin, gn_g.reshape(depth, 1, rw))


def _ffn_up_body(tiles_per_seq, n_sub, h_ref, hp_ref, hn_ref, ssq_ref, ssqp_ref, ssqn_ref,
                 wg_ref, wu_ref, cw_ref, cb_ref, wd_ref, o_ref, wd_bf16_ref):
    @pl.when(pl.program_id(0) == 0)
    def _():
        wd_bf16_ref[...] = wd_ref[...].astype(wd_bf16_ref.dtype)

    tm, d_model = h_ref.shape
    halo = hp_ref.shape[0]
    wg = wg_ref[...].astype(_BF16)
    wu = wu_ref[...].astype(_BF16)
    w = cw_ref[...]
    bias = cb_ref[...]
    t = pl.program_id(0) % tiles_per_seq
    scale_prev = _row_scale(ssqp_ref[...], d_model)[-1:, :] * (t != 0).astype(_F32)
    scale_next = _row_scale(ssqn_ref[...], d_model)[0:1, :] * (t != tiles_per_seq - 1).astype(_F32)

    sub = tm // n_sub
    rows = lambda s: slice(s * sub, (s + 1) * sub)
    scale = lambda s: _row_scale(ssq_ref[rows(s), :], d_model)

    def gate_pre(s):
        parts = ([hp_ref[...]] if s == 0 else []) + [h_ref[rows(s), :]] + ([hn_ref[...]] if s == n_sub - 1 else [])
        g = _dot(parts[0] if len(parts) == 1 else jnp.concatenate(parts, axis=0), wg)
        lo = halo if s == 0 else 0
        before = g[lo - 1:lo, :] * scale_prev if s == 0 else None
        after = g[lo + sub:lo + sub + 1, :] * scale_next if s == n_sub - 1 else None
        return g[lo:lo + sub, :] * scale(s), before, after

    g_cur, prev_row, g_next = gate_pre(0)
    for s in range(n_sub):
        if s + 1 < n_sub:
            g_after, _, g_next = gate_pre(s + 1)
            next_row = g_after[0:1, :]
        else:
            next_row = g_next
        u = _dot(h_ref[rows(s), :], wu) * scale(s)
        g_up, g_dn = _neighbour_rows(g_cur, prev_row, next_row)
        gate = g_up * w[0:1, :] + g_cur * w[1:2, :] + g_dn * w[2:3, :] + bias
        o_ref[rows(s), :] = (jax.nn.silu(gate) * u).astype(o_ref.dtype)
        prev_row = g_cur[sub - 1:sub, :]
        if s + 1 < n_sub:
            g_cur = g_after


def _ffn_up(h, ssq, w_gate, w_up, conv_w, conv_b, w_down, layer, seq, tm=2048, tn=256, sub=512):
    m, k = h.shape
    depth, _, ff = w_gate.shape
    d_out = w_down.shape[2]
    tm, tn = _fit(seq, tm), _fit(ff, tn)
    n_col = ff // tn
    halo = BF16_SUBLANES
    f32_rows = 8
    once = dict(pipeline_mode=pl.Buffered(1))
    before = lambda rows: (lambda i, j: (jnp.maximum(i * (tm // rows) - 1, 0), 0))
    after = lambda rows: (lambda i, j: (jnp.minimum((i + 1) * (tm // rows), m // rows - 1), 0))
    weight = pl.BlockSpec((None, k, tn), lambda i, j: (layer, 0, j))
    vec = lambda rows: pl.BlockSpec((None, rows, tn), lambda i, j: (layer, 0, j))
    return pl.pallas_call(
        functools.partial(_ffn_up_body, seq // tm, tm // _fit(tm, sub)),
        grid=(m // tm, ff // tn),
        in_specs=[pl.BlockSpec((tm, k), lambda i, j: (i, 0), **once),
                  pl.BlockSpec((halo, k), before(halo)),
                  pl.BlockSpec((halo, k), after(halo)),
                  pl.BlockSpec((tm, SSQ_LANES), lambda i, j: (i, 0), **once),
                  pl.BlockSpec((f32_rows, SSQ_LANES), before(f32_rows)),
                  pl.BlockSpec((f32_rows, SSQ_LANES), after(f32_rows)),
                  weight, weight, vec(3), vec(1),
                  pl.BlockSpec((None, tn, d_out), lambda i, j: (layer, jnp.where(i == 0, j, 0), 0))],
        out_specs=[pl.BlockSpec((tm, tn), lambda i, j: (i, j)),
                   pl.BlockSpec((tn, d_out), lambda i, j: (jnp.where(i == 0, j, n_col - 1), 0))],
        out_shape=[jax.ShapeDtypeStruct((m, ff), _BF16), jax.ShapeDtypeStruct((ff, d_out), _BF16)],
        compiler_params=_params("arbitrary", "arbitrary"),
        name="ffn_up",
    )(h, h, h, ssq, ssq, ssq, w_gate, w_up, conv_w, conv_b.reshape(depth, 1, ff), w_down)


def kernel(x, positions, norm1_g, w_in, conv_a_w, conv_a_b, beta_a, ret_decay_logit, ret_gn_g, w_out,
           norm2_g, w_gate, w_up, ffn_conv_w, ffn_conv_b, w_down, norm_f_g):
    batch, seq, d_model = x.shape
    depth = w_in.shape[0]
    conv_width = conv_a_w.shape[-1]
    m = batch * seq

    xf = x.reshape(m, d_model)
    cos, sin = _rope_tables(positions)
    xg, ssq = _stream_in(xf, norm1_g[0])

    for l in range(depth):
        proj = _in_proj(xg, ssq, w_in, l)
        ya = _conv_mixer(proj, conv_a_w, conv_a_b, beta_a, l, seq)
        yr = _retention(proj, cos, sin, ret_decay_logit, ret_gn_g, l, batch, seq, 3 * conv_width)
        xf, xg, ssq = _out_proj(ya, yr, w_out, l, xf, norm2_g[l])
        hidden, w_dn = _ffn_up(xg, ssq, w_gate, w_up, ffn_conv_w, ffn_conv_b, w_down, l, seq)
        if l + 1 < depth:
            xf, xg, ssq = _down_proj(hidden, w_dn, xf, norm1_g[l + 1])
        else:
            xf = _down_proj(hidden, w_dn, xf)

    return _rmsnorm(xf, norm_f_g, x.dtype).reshape(batch, seq, d_model)
```

```python
import functools

import jax
import jax.numpy as jnp
from jax import lax
from jax.experimental import pallas as pl
from jax.experimental.pallas import tpu as pltpu

EPS = 1e-6
ROPE_BASE = 10000.0
HEAD_DIM = 256
CONV_GROUP = 128
CHUNK = 256
BF16_SUBLANES = 16
V7X_VMEM_LIMIT = 56 * 1024 * 1024

_BF16 = jnp.bfloat16
_F32 = jnp.float32


def _params(*sem):
    return pltpu.CompilerParams(dimension_semantics=sem, vmem_limit_bytes=V7X_VMEM_LIMIT)


def _dot(a, b):
    return jnp.dot(a, b, preferred_element_type=_F32)


def _fit(dim, tile):
    return tile if dim % tile == 0 else dim


SSQ_LANES = 128


def _row_scale(ssq, width):
    return lax.rsqrt(jnp.sum(ssq, axis=-1, keepdims=True) / width + EPS)


def _emit_norm_inputs(x_new, gain_ref, xg_ref, ssq_ref, col_step):
    xg_ref[...] = (x_new * gain_ref[...]).astype(xg_ref.dtype)
    sq = x_new * x_new
    part = sq[:, 0:SSQ_LANES]
    for c in range(1, sq.shape[1] // SSQ_LANES):
        part = part + sq[:, c * SSQ_LANES:(c + 1) * SSQ_LANES]

    @pl.when(col_step == 0)
    def _():
        ssq_ref[...] = part

    @pl.when(col_step != 0)
    def _():
        ssq_ref[...] += part


def _norm_inputs_out(m, n, tm, tn):
    specs = [pl.BlockSpec((tm, tn), lambda i, j: (i, j)), pl.BlockSpec((tm, SSQ_LANES), lambda i, j: (i, 0))]
    shapes = [jax.ShapeDtypeStruct((m, n), _BF16), jax.ShapeDtypeStruct((m, SSQ_LANES), _F32)]
    return specs, shapes


def _stream_in_body(x_ref, g_ref, xg_ref, ssq_ref):
    _emit_norm_inputs(x_ref[...], g_ref, xg_ref, ssq_ref, pl.program_id(1))


def _stream_in(x, gain, tm=256):
    m, d = x.shape
    tm = _fit(m, tm)
    specs, shapes = _norm_inputs_out(m, d, tm, d)
    return pl.pallas_call(
        _stream_in_body,
        grid=(m // tm, 1),
        in_specs=[pl.BlockSpec((tm, d), lambda i, j: (i, 0)),
                  pl.BlockSpec((1, d), lambda i, j: (0, 0))],
        out_specs=specs,
        out_shape=shapes,
        compiler_params=_params("parallel", "arbitrary"),
        name="stream_in",
    )(x, gain.reshape(1, d))


def _rmsnorm_body(x_ref, g_ref, o_ref):
    x = x_ref[...]
    ms = jnp.mean(x * x, axis=-1, keepdims=True)
    o_ref[...] = (x * lax.rsqrt(ms + EPS) * g_ref[...]).astype(o_ref.dtype)


def _rmsnorm(x, g, out_dtype, tm=256):
    m, d = x.shape
    tm = _fit(m, tm)
    return pl.pallas_call(
        _rmsnorm_body,
        grid=(m // tm,),
        in_specs=[pl.BlockSpec((tm, d), lambda i: (i, 0)),
                  pl.BlockSpec((1, d), lambda i: (0, 0))],
        out_specs=pl.BlockSpec((tm, d), lambda i: (i, 0)),
        out_shape=jax.ShapeDtypeStruct((m, d), out_dtype),
        compiler_params=_params("parallel"),
        name="rmsnorm",
    )(x, g.reshape(1, d))


def _in_proj_body(a_ref, ssq_ref, wslice_ref, o_ref, w_bf16_ref):
    j = pl.program_id(0)
    i = pl.program_id(1)
    n_col = pl.num_programs(0) - 1
    rows = wslice_ref.shape[0]

    @pl.when(j < n_col)
    def _():
        dst = pl.ds(pl.multiple_of(i * rows, rows), rows)
        w_bf16_ref[j % 2, dst, :] = wslice_ref[...].astype(w_bf16_ref.dtype)

    @pl.when(j > 0)
    def _():
        scale = _row_scale(ssq_ref[...], a_ref.shape[1])
        o_ref[...] = (_dot(a_ref[...], w_bf16_ref[(j - 1) % 2]) * scale).astype(o_ref.dtype)


def _in_proj(a, ssq, w, layer, tm=1024, tn=1024):
    m, k = a.shape
    n = w.shape[2]
    tm, tn = _fit(m, tm), _fit(n, tn)
    n_row, n_col = m // tm, n // tn
    rows = k // n_row
    assert rows * n_row == k and rows % BF16_SUBLANES == 0
    row_tile = lambda j, i: jnp.where(j > 0, i, 0)
    return pl.pallas_call(
        _in_proj_body,
        grid=(n_col + 1, n_row),
        in_specs=[pl.BlockSpec((tm, k), lambda j, i: (row_tile(j, i), 0)),
                  pl.BlockSpec((tm, SSQ_LANES), lambda j, i: (row_tile(j, i), 0)),
                  pl.BlockSpec((None, rows, tn), lambda j, i: (layer, i, jnp.minimum(j, n_col - 1)))],
        out_specs=pl.BlockSpec((tm, tn), lambda j, i: (row_tile(j, i), jnp.maximum(j - 1, 0))),
        out_shape=jax.ShapeDtypeStruct((m, n), _BF16),
        scratch_shapes=[pltpu.VMEM((2, k, tn), _BF16)],
        compiler_params=_params("arbitrary", "arbitrary"),
        name="in_proj",
    )(a, ssq, w)


def _out_proj_body(ya_ref, yr_ref, wa_ref, wr_ref, x_ref, gain_ref, o_ref, xg_ref, ssq_ref):
    y = _dot(ya_ref[...], wa_ref[...]) + _dot(yr_ref[...], wr_ref[...])
    x_new = x_ref[...] + y
    o_ref[...] = x_new
    _emit_norm_inputs(x_new, gain_ref, xg_ref, ssq_ref, pl.program_id(1))


def _out_proj(ya, yr, w_tiles, x, gain, tm=1024):
    m, ka = ya.shape
    kr = yr.shape[1]
    n_col, _, tn = w_tiles.shape
    n = n_col * tn
    assert ka == kr
    tm = _fit(m, tm)
    tile = pl.BlockSpec((tm, tn), lambda i, j: (i, j))
    specs, shapes = _norm_inputs_out(m, n, tm, tn)
    return pl.pallas_call(
        _out_proj_body,
        grid=(m // tm, n // tn),
        in_specs=[pl.BlockSpec((tm, ka), lambda i, j: (i, 0)),
                  pl.BlockSpec((tm, kr), lambda i, j: (i, 0)),
                  pl.BlockSpec((None, ka, tn), lambda i, j: (j, 0, 0)),
                  pl.BlockSpec((None, kr, tn), lambda i, j: (j, 1, 0)),
                  tile,
                  pl.BlockSpec((1, tn), lambda i, j: (0, j))],
        out_specs=[tile] + specs,
        out_shape=[jax.ShapeDtypeStruct((m, n), _F32)] + shapes,
        compiler_params=_params("parallel", "arbitrary"),
        name="out_proj",
    )(ya, yr, w_tiles, w_tiles, x, gain.reshape(1, n))


def _down_proj_body(a_ref, w_ref, x_ref, gain_ref, o_ref, xg_ref, ssq_ref):
    x_new = x_ref[...] + _dot(a_ref[...], w_ref[...])
    o_ref[...] = x_new
    _emit_norm_inputs(x_new, gain_ref, xg_ref, ssq_ref, pl.program_id(1))


def _down_proj_last_body(a_ref, w_ref, x_ref, o_ref):
    o_ref[...] = x_ref[...] + _dot(a_ref[...], w_ref[...])


def _down_proj(a, w_tiles, x, gain=None, tm=512):
    m, k = a.shape
    n_col, _, tn = w_tiles.shape
    n = n_col * tn
    tm = _fit(m, tm)
    tile = pl.BlockSpec((tm, tn), lambda i, j: (i, j))
    in_specs = [pl.BlockSpec((tm, k), lambda i, j: (i, 0)), pl.BlockSpec((None, k, tn), lambda i, j: (j, 0, 0)), tile]
    if gain is None:
        return pl.pallas_call(
            _down_proj_last_body,
            grid=(m // tm, n // tn),
            in_specs=in_specs,
            out_specs=tile,
            out_shape=jax.ShapeDtypeStruct((m, n), _F32),
            compiler_params=_params("parallel", "parallel"),
            name="down_proj_last",
        )(a, w_tiles, x)
    specs, shapes = _norm_inputs_out(m, n, tm, tn)
    return pl.pallas_call(
        _down_proj_body,
        grid=(m // tm, n // tn),
        in_specs=in_specs + [pl.BlockSpec((1, tn), lambda i, j: (0, j))],
        out_specs=[tile] + specs,
        out_shape=[jax.ShapeDtypeStruct((m, n), _F32)] + shapes,
        compiler_params=_params("parallel", "arbitrary"),
        name="down_proj",
    )(a, w_tiles, x, gain.reshape(1, n))


PROJ_TILE = 512


def _store_weight_tiles(w_ref, tiles_ref):
    tn = tiles_ref.shape[2]
    for c in range(tiles_ref.shape[0]):
        tiles_ref[c] = w_ref[:, c * tn:(c + 1) * tn].astype(tiles_ref.dtype)


def _neighbour_rows(x, prev_row, next_row):
    tm = x.shape[0]
    rows = lax.broadcasted_iota(jnp.int32, x.shape, 0)
    up = jnp.where(rows == 0, prev_row, pltpu.roll(x, 1, 0))
    dn = jnp.where(rows == tm - 1, next_row, pltpu.roll(x, tm - 1, 0))
    return up, dn


def _halo_specs(tm, tc, col_block, rows_total, halo=BF16_SUBLANES):
    per = tm // halo
    last = rows_total // halo - 1
    prev = pl.BlockSpec((halo, tc), lambda i, c: (jnp.maximum(i * per - 1, 0), col_block(c)))
    nxt = pl.BlockSpec((halo, tc), lambda i, c: (jnp.minimum((i + 1) * per, last), col_block(c)))
    return prev, nxt


def _conv_mixer_body(tiles_per_seq, b_ref, c_ref, x_ref, cp_ref, xp_ref, cn_ref, xn_ref,
                     w_ref, bias_ref, beta_ref, o_ref):
    f = lambda r: r[...].astype(_F32)
    u = f(c_ref) * f(x_ref)
    t = pl.program_id(0) % tiles_per_seq
    keep_prev = (t != 0).astype(_F32)
    keep_next = (t != tiles_per_seq - 1).astype(_F32)
    halo = cp_ref.shape[0]
    u_prev = (f(cp_ref) * f(xp_ref))[halo - 1:halo, :] * keep_prev
    u_next = (f(cn_ref) * f(xn_ref))[0:1, :] * keep_next
    up, dn = _neighbour_rows(u, u_prev, u_next)
    w = w_ref[...]
    conv = up * w[0:1, :] + u * w[1:2, :] + dn * w[2:3, :] + bias_ref[...]
    ya = f(b_ref) * conv
    tm, tc = ya.shape
    beta = beta_ref[...]
    for g in range(tc // CONV_GROUP):
        sl = slice(g * CONV_GROUP, (g + 1) * CONV_GROUP)
        yg = ya[:, sl]
        ms = jnp.mean(yg * yg, axis=-1, keepdims=True)
        o_ref[:, sl] = (yg * lax.rsqrt(ms + EPS) * beta[:, sl]).astype(o_ref.dtype)


def _conv_mixer(proj, conv_w, conv_b, beta, layer, seq, tm=1024, tc=1024):
    m = proj.shape[0]
    cw = conv_w.shape[-1]
    tm, tc = _fit(seq, tm), _fit(cw, tc)
    nb = cw // tc
    col = lambda off: (lambda c: c + off * nb)
    main = lambda off: pl.BlockSpec((tm, tc), lambda i, c: (i, c + off * nb))
    cp, cn = _halo_specs(tm, tc, col(1), m)
    xp, xn = _halo_specs(tm, tc, col(2), m)
    vec = lambda rows: pl.BlockSpec((None, rows, tc), lambda i, c: (layer, 0, c))
    depth = conv_w.shape[0]
    return pl.pallas_call(
        functools.partial(_conv_mixer_body, seq // tm),
        grid=(m // tm, nb),
        in_specs=[main(0), main(1), main(2), cp, xp, cn, xn, vec(3), vec(1), vec(1)],
        out_specs=pl.BlockSpec((tm, tc), lambda i, c: (i, c)),
        out_shape=jax.ShapeDtypeStruct((m, cw), _BF16),
        compiler_params=_params("parallel", "parallel"),
        name="conv_mixer",
    )(proj, proj, proj, proj, proj, proj, proj, conv_w, conv_b.reshape(depth, 1, cw), beta.reshape(depth, 1, cw))


def _rope_tables_body(pos_ref, inv_ref, cos_ref, sin_ref):
    ang = pos_ref[...].astype(_F32) * inv_ref[...]
    cos_ref[...] = jnp.cos(ang)
    sin_ref[...] = jnp.sin(ang)


def _rope_tables(positions, tm=1024):
    m = positions.size
    tm = _fit(m, tm)
    half = HEAD_DIM // 2
    inv_freq = ROPE_BASE ** (-jnp.arange(half, dtype=_F32) / half)
    out = jax.ShapeDtypeStruct((m, half), _F32)
    return pl.pallas_call(
        _rope_tables_body,
        grid=(m // tm,),
        in_specs=[pl.BlockSpec((tm, 1), lambda i: (i, 0)),
                  pl.BlockSpec((1, half), lambda i: (0, 0))],
        out_specs=[pl.BlockSpec((tm, half), lambda i: (i, 0))] * 2,
        out_shape=[out, out],
        compiler_params=_params("parallel"),
        name="rope_tables",
    )(positions.reshape(m, 1), inv_freq.reshape(1, half))


def _rotary(t, cos, sin):
    half = t.shape[-1] // 2
    t1, t2 = t[:, :half], t[:, half:]
    return jnp.concatenate([t1 * cos - t2 * sin, t2 * cos + t1 * sin], axis=-1)


def _retention_body(n_heads, logit_ref, q_ref, k_ref, v_ref, g_ref, cos_ref, sin_ref, gn_ref, wo_ref,
                    o_ref, wo_tiles_ref, krot_ref, kvf_ref, kvb_ref, sf_ref, sb_ref, state_ref):
    _store_weight_tiles(wo_ref, wo_tiles_ref)

    h = pl.program_id(1)
    seq, d = q_ref.shape
    c = CHUNK
    n_chunks = seq // c

    def log_gamma(shape, direction):
        return jax.nn.log_sigmoid(jnp.full(shape, logit_ref[direction * n_heads + h], _F32))

    def chunk_rows(n):
        return pl.ds(pl.multiple_of(n * c, c), c)

    pos_cd = lax.broadcasted_iota(jnp.int32, (c, d), 0).astype(_F32)
    lf = log_gamma((c, d), 0)
    lb = log_gamma((c, d), 1)

    wk_f = jnp.exp(lf * (c - 1.0 - pos_cd))
    wk_b = jnp.exp(lb * pos_cd)

    def summarise(n, carry):
        rows = chunk_rows(n)
        k = _rotary(k_ref[rows, :].astype(_F32), cos_ref[rows, :], sin_ref[rows, :])
        krot_ref[rows, :] = k.astype(_BF16)
        v = v_ref[rows, :]
        contract_rows = (((0,), (0,)), ((), ()))
        kvf_ref[n] = lax.dot_general((k * wk_f).astype(_BF16), v, contract_rows, preferred_element_type=_F32)
        kvb_ref[n] = lax.dot_general((k * wk_b).astype(_BF16), v, contract_rows, preferred_element_type=_F32)
        return carry

    lax.fori_loop(0, n_chunks, summarise, 0, unroll=4)

    def scan(summary_ref, out_ref, log_g, order):
        g_chunk = jnp.exp(log_g * float(c))
        state_ref[...] = jnp.zeros_like(state_ref)

        def step(t, carry):
            n = order(t)
            state = state_ref[...]
            out_ref[n] = state.astype(_BF16)
            state_ref[...] = state * g_chunk + summary_ref[n]
            return carry

        lax.fori_loop(0, n_chunks, step, 0, unroll=4)

    scan(kvf_ref, sf_ref, log_gamma((1, d), 0), lambda t: t)
    scan(kvb_ref, sb_ref, log_gamma((1, d), 1), lambda t: n_chunks - 1 - t)

    row = lax.broadcasted_iota(jnp.int32, (c, c), 0)
    col = lax.broadcasted_iota(jnp.int32, (c, c), 1)
    rel = (row - col).astype(_F32)
    scale = HEAD_DIM ** -0.5
    dec = jnp.where(rel >= 0,
                    jnp.exp(log_gamma((c, c), 0) * jnp.maximum(rel, 0.0)),
                    jnp.exp(log_gamma((c, c), 1) * jnp.maximum(-rel, 0.0))) * scale
    wq_f = jnp.exp(lf * (pos_cd + 1.0)) * scale
    wq_b = jnp.exp(lb * (c - pos_cd)) * scale
    gain = gn_ref[...]

    def output(n, carry):
        rows = chunk_rows(n)
        q = _rotary(q_ref[rows, :].astype(_F32), cos_ref[rows, :], sin_ref[rows, :]).astype(_BF16)
        scores = lax.dot_general(q, krot_ref[rows, :], (((1,), (1,)), ((), ())),
                                 preferred_element_type=_F32) * dec
        o = _dot(scores.astype(_BF16), v_ref[rows, :])
        o = o + wq_f * _dot(q, sf_ref[n]) + wq_b * _dot(q, sb_ref[n])
        mu = jnp.mean(o, axis=-1, keepdims=True)
        cen = o - mu
        var = jnp.mean(cen * cen, axis=-1, keepdims=True)
        normed = cen * lax.rsqrt(var + EPS) * gain
        gate = g_ref[rows, :].astype(_F32)
        o_ref[rows, :] = (normed * jax.nn.silu(gate)).astype(o_ref.dtype)
        return carry

    lax.fori_loop(0, n_chunks, output, 0, unroll=4)


def _retention(proj, cos, sin, decay_logit, gn_g, w_out, layer, batch, seq, col0):
    m = proj.shape[0]
    depth, rw = gn_g.shape
    n_heads = rw // HEAD_DIM
    _, wo_rows, wo_cols = w_out.shape
    tn = _fit(wo_cols, PROJ_TILE)
    slab = wo_rows // (batch * n_heads)
    assert slab * batch * n_heads == wo_rows and slab % BF16_SUBLANES == 0
    first = col0 // HEAD_DIM
    n_chunks = seq // CHUNK
    part = lambda p: pl.BlockSpec((seq, HEAD_DIM), lambda b, h: (b, first + p * n_heads + h))
    table = pl.BlockSpec((seq, HEAD_DIM // 2), lambda b, h: (b, 0), pipeline_mode=pl.Buffered(1))
    summaries = pltpu.VMEM((n_chunks, HEAD_DIM, HEAD_DIM), _F32)
    states = pltpu.VMEM((n_chunks, HEAD_DIM, HEAD_DIM), _BF16)
    return pl.pallas_call(
        functools.partial(_retention_body, n_heads),
        grid=(batch, n_heads),
        in_specs=[pl.BlockSpec(memory_space=pltpu.SMEM),
                  part(0), part(1), part(2), part(3), table, table,
                  pl.BlockSpec((None, 1, HEAD_DIM), lambda b, h: (layer, 0, h)),
                  pl.BlockSpec((None, slab, wo_cols), lambda b, h: (layer, b * n_heads + h, 0))],
        out_specs=[pl.BlockSpec((seq, HEAD_DIM), lambda b, h: (b, h)),
                   pl.BlockSpec((wo_cols // tn, slab, tn), lambda b, h: (0, b * n_heads + h, 0))],
        out_shape=[jax.ShapeDtypeStruct((m, rw), _BF16),
                   jax.ShapeDtypeStruct((wo_cols // tn, wo_rows, tn), _BF16)],
        scratch_shapes=[pltpu.VMEM((seq, HEAD_DIM), _BF16), summaries, summaries, states, states,
                        pltpu.VMEM((HEAD_DIM, HEAD_DIM), _F32)],
        compiler_params=_params("parallel", "parallel"),
        name="retention",
    )(decay_logit[layer].reshape(-1), proj, proj, proj, proj, cos, sin, gn_g.reshape(depth, 1, rw), w_out)


def _ffn_up_body(tiles_per_seq, n_sub, h_ref, hp_ref, hn_ref, ssq_ref, ssqp_ref, ssqn_ref,
                 wg_ref, wu_ref, cw_ref, cb_ref, wd_ref, o_ref, wd_tiles_ref):
    @pl.when(pl.program_id(0) == 0)
    def _():
        _store_weight_tiles(wd_ref, wd_tiles_ref)

    tm, d_model = h_ref.shape
    halo = hp_ref.shape[0]
    wg = wg_ref[...].astype(_BF16)
    wu = wu_ref[...].astype(_BF16)
    w = cw_ref[...]
    bias = cb_ref[...]
    t = pl.program_id(0) % tiles_per_seq
    scale_prev = _row_scale(ssqp_ref[...], d_model)[-1:, :] * (t != 0).astype(_F32)
    scale_next = _row_scale(ssqn_ref[...], d_model)[0:1, :] * (t != tiles_per_seq - 1).astype(_F32)

    sub = tm // n_sub
    rows = lambda s: slice(s * sub, (s + 1) * sub)
    scale = lambda s: _row_scale(ssq_ref[rows(s), :], d_model)

    def gate_pre(s):
        parts = ([hp_ref[...]] if s == 0 else []) + [h_ref[rows(s), :]] + ([hn_ref[...]] if s == n_sub - 1 else [])
        g = _dot(parts[0] if len(parts) == 1 else jnp.concatenate(parts, axis=0), wg)
        lo = halo if s == 0 else 0
        before = g[lo - 1:lo, :] * scale_prev if s == 0 else None
        after = g[lo + sub:lo + sub + 1, :] * scale_next if s == n_sub - 1 else None
        return g[lo:lo + sub, :] * scale(s), before, after

    g_cur, prev_row, g_next = gate_pre(0)
    for s in range(n_sub):
        if s + 1 < n_sub:
            g_after, _, g_next = gate_pre(s + 1)
            next_row = g_after[0:1, :]
        else:
            next_row = g_next
        u = _dot(h_ref[rows(s), :], wu) * scale(s)
        g_up, g_dn = _neighbour_rows(g_cur, prev_row, next_row)
        gate = g_up * w[0:1, :] + g_cur * w[1:2, :] + g_dn * w[2:3, :] + bias
        o_ref[rows(s), :] = (jax.nn.silu(gate) * u).astype(o_ref.dtype)
        prev_row = g_cur[sub - 1:sub, :]
        if s + 1 < n_sub:
            g_cur = g_after


def _ffn_up(h, ssq, w_gate, w_up, conv_w, conv_b, w_down, layer, seq, tm=2048, tn=256, sub=512):
    m, k = h.shape
    depth, _, ff = w_gate.shape
    d_out = w_down.shape[2]
    tm, tn = _fit(seq, tm), _fit(ff, tn)
    n_col = ff // tn
    dt = _fit(d_out, PROJ_TILE)
    halo = BF16_SUBLANES
    f32_rows = 8
    once = dict(pipeline_mode=pl.Buffered(1))
    before = lambda rows: (lambda i, j: (jnp.maximum(i * (tm // rows) - 1, 0), 0))
    after = lambda rows: (lambda i, j: (jnp.minimum((i + 1) * (tm // rows), m // rows - 1), 0))
    weight = pl.BlockSpec((None, k, tn), lambda i, j: (layer, 0, j))
    vec = lambda rows: pl.BlockSpec((None, rows, tn), lambda i, j: (layer, 0, j))
    return pl.pallas_call(
        functools.partial(_ffn_up_body, seq // tm, tm // _fit(tm, sub)),
        grid=(m // tm, ff // tn),
        in_specs=[pl.BlockSpec((tm, k), lambda i, j: (i, 0), **once),
                  pl.BlockSpec((halo, k), before(halo)),
                  pl.BlockSpec((halo, k), after(halo)),
                  pl.BlockSpec((tm, SSQ_LANES), lambda i, j: (i, 0), **once),
                  pl.BlockSpec((f32_rows, SSQ_LANES), before(f32_rows)),
                  pl.BlockSpec((f32_rows, SSQ_LANES), after(f32_rows)),
                  weight, weight, vec(3), vec(1),
                  pl.BlockSpec((None, tn, d_out), lambda i, j: (layer, jnp.where(i == 0, j, 0), 0))],
        out_specs=[pl.BlockSpec((tm, tn), lambda i, j: (i, j)),
                   pl.BlockSpec((d_out // dt, tn, dt), lambda i, j: (0, jnp.where(i == 0, j, n_col - 1), 0))],
        out_shape=[jax.ShapeDtypeStruct((m, ff), _BF16), jax.ShapeDtypeStruct((d_out // dt, ff, dt), _BF16)],
        compiler_params=_params("arbitrary", "arbitrary"),
        name="ffn_up",
    )(h, h, h, ssq, ssq, ssq, w_gate, w_up, conv_w, conv_b.reshape(depth, 1, ff), w_down)


def kernel(x, positions, norm1_g, w_in, conv_a_w, conv_a_b, beta_a, ret_decay_logit, ret_gn_g, w_out,
           norm2_g, w_gate, w_up, ffn_conv_w, ffn_conv_b, w_down, norm_f_g):
    batch, seq, d_model = x.shape
    depth = w_in.shape[0]
    conv_width = conv_a_w.shape[-1]
    m = batch * seq

    xf = x.reshape(m, d_model)
    cos, sin = _rope_tables(positions)
    xg, ssq = _stream_in(xf, norm1_g[0])

    for l in range(depth):
        proj = _in_proj(xg, ssq, w_in, l)
        ya = _conv_mixer(proj, conv_a_w, conv_a_b, beta_a, l, seq)
        yr, w_o = _retention(proj, cos, sin, ret_decay_logit, ret_gn_g, w_out, l, batch, seq, 3 * conv_width)
        xf, xg, ssq = _out_proj(ya, yr, w_o, xf, norm2_g[l])
        hidden, w_dn = _ffn_up(xg, ssq, w_gate, w_up, ffn_conv_w, ffn_conv_b, w_down, l, seq)
        if l + 1 < depth:
            xf, xg, ssq = _down_proj(hidden, w_dn, xf, norm1_g[l + 1])
        else:
            xf = _down_proj(hidden, w_dn, xf)

    return _rmsnorm(xf, norm_f_g, x.dtype).reshape(batch, seq, d_model)
```

```python
import functools

import jax
import jax.numpy as jnp
from jax import lax
from jax.experimental import pallas as pl
from jax.experimental.pallas import tpu as pltpu

EPS = 1e-6
ROPE_BASE = 10000.0
HEAD_DIM = 256
CONV_GROUP = 128
CHUNK = 256
BF16_SUBLANES = 16
V7X_VMEM_LIMIT = 56 * 1024 * 1024

_BF16 = jnp.bfloat16
_F32 = jnp.float32


def _params(*sem):
    return pltpu.CompilerParams(dimension_semantics=sem, vmem_limit_bytes=V7X_VMEM_LIMIT)


def _dot(a, b):
    return jnp.dot(a, b, preferred_element_type=_F32)


def _fit(dim, tile):
    return tile if dim % tile == 0 else dim


SSQ_LANES = 128


def _row_scale(ssq, width):
    return lax.rsqrt(jnp.sum(ssq, axis=-1, keepdims=True) / width + EPS)


def _emit_norm_inputs(x_new, gain_ref, xg_ref, ssq_ref, col_step):
    xg_ref[...] = (x_new * gain_ref[...]).astype(xg_ref.dtype)
    sq = x_new * x_new
    part = sq[:, 0:SSQ_LANES]
    for c in range(1, sq.shape[1] // SSQ_LANES):
        part = part + sq[:, c * SSQ_LANES:(c + 1) * SSQ_LANES]

    @pl.when(col_step == 0)
    def _():
        ssq_ref[...] = part

    @pl.when(col_step != 0)
    def _():
        ssq_ref[...] += part


def _norm_inputs_out(m, n, tm, tn):
    specs = [pl.BlockSpec((tm, tn), lambda i, j: (i, j)), pl.BlockSpec((tm, SSQ_LANES), lambda i, j: (i, 0))]
    shapes = [jax.ShapeDtypeStruct((m, n), _BF16), jax.ShapeDtypeStruct((m, SSQ_LANES), _F32)]
    return specs, shapes


def _stream_in_body(x_ref, g_ref, xg_ref, ssq_ref):
    _emit_norm_inputs(x_ref[...], g_ref, xg_ref, ssq_ref, pl.program_id(1))


def _stream_in(x, gain, tm=256):
    m, d = x.shape
    tm = _fit(m, tm)
    specs, shapes = _norm_inputs_out(m, d, tm, d)
    return pl.pallas_call(
        _stream_in_body,
        grid=(m // tm, 1),
        in_specs=[pl.BlockSpec((tm, d), lambda i, j: (i, 0)),
                  pl.BlockSpec((1, d), lambda i, j: (0, 0))],
        out_specs=specs,
        out_shape=shapes,
        compiler_params=_params("parallel", "arbitrary"),
        name="stream_in",
    )(x, gain.reshape(1, d))


def _rmsnorm_body(x_ref, g_ref, o_ref):
    x = x_ref[...]
    ms = jnp.mean(x * x, axis=-1, keepdims=True)
    o_ref[...] = (x * lax.rsqrt(ms + EPS) * g_ref[...]).astype(o_ref.dtype)


def _rmsnorm(x, g, out_dtype, tm=256):
    m, d = x.shape
    tm = _fit(m, tm)
    return pl.pallas_call(
        _rmsnorm_body,
        grid=(m // tm,),
        in_specs=[pl.BlockSpec((tm, d), lambda i: (i, 0)),
                  pl.BlockSpec((1, d), lambda i: (0, 0))],
        out_specs=pl.BlockSpec((tm, d), lambda i: (i, 0)),
        out_shape=jax.ShapeDtypeStruct((m, d), out_dtype),
        compiler_params=_params("parallel"),
        name="rmsnorm",
    )(x, g.reshape(1, d))


def _in_proj_body(a_ref, ssq_ref, wslice_ref, o_ref, w_bf16_ref):
    j = pl.program_id(0)
    i = pl.program_id(1)
    n_col = pl.num_programs(0) - 1
    rows = wslice_ref.shape[0]

    @pl.when(j < n_col)
    def _():
        dst = pl.ds(pl.multiple_of(i * rows, rows), rows)
        w_bf16_ref[j % 2, dst, :] = wslice_ref[...].astype(w_bf16_ref.dtype)

    @pl.when(j > 0)
    def _():
        scale = _row_scale(ssq_ref[...], a_ref.shape[1])
        o_ref[...] = (_dot(a_ref[...], w_bf16_ref[(j - 1) % 2]) * scale).astype(o_ref.dtype)


def _in_proj(a, ssq, w, layer, tm=1024, tn=1024):
    m, k = a.shape
    n = w.shape[2]
    tm, tn = _fit(m, tm), _fit(n, tn)
    n_row, n_col = m // tm, n // tn
    rows = k // n_row
    assert rows * n_row == k and rows % BF16_SUBLANES == 0
    row_tile = lambda j, i: jnp.where(j > 0, i, 0)
    return pl.pallas_call(
        _in_proj_body,
        grid=(n_col + 1, n_row),
        in_specs=[pl.BlockSpec((tm, k), lambda j, i: (row_tile(j, i), 0)),
                  pl.BlockSpec((tm, SSQ_LANES), lambda j, i: (row_tile(j, i), 0)),
                  pl.BlockSpec((None, rows, tn), lambda j, i: (layer, i, jnp.minimum(j, n_col - 1)))],
        out_specs=pl.BlockSpec((tm, tn), lambda j, i: (row_tile(j, i), jnp.maximum(j - 1, 0))),
        out_shape=jax.ShapeDtypeStruct((m, n), _BF16),
        scratch_shapes=[pltpu.VMEM((2, k, tn), _BF16)],
        compiler_params=_params("arbitrary", "arbitrary"),
        name="in_proj",
    )(a, ssq, w)


def _out_proj_body(ya_ref, yr_ref, wa_ref, wr_ref, x_ref, gain_ref, o_ref, xg_ref, ssq_ref):
    y = _dot(ya_ref[...], wa_ref[...]) + _dot(yr_ref[...], wr_ref[...])
    x_new = x_ref[...] + y
    o_ref[...] = x_new
    _emit_norm_inputs(x_new, gain_ref, xg_ref, ssq_ref, pl.program_id(1))


def _out_proj(ya, yr, w_tiles, x, gain, tm=1024):
    m, ka = ya.shape
    kr = yr.shape[1]
    n_col, _, tn = w_tiles.shape
    n = n_col * tn
    assert ka == kr
    tm = _fit(m, tm)
    tile = pl.BlockSpec((tm, tn), lambda i, j: (i, j))
    specs, shapes = _norm_inputs_out(m, n, tm, tn)
    return pl.pallas_call(
        _out_proj_body,
        grid=(m // tm, n // tn),
        in_specs=[pl.BlockSpec((tm, ka), lambda i, j: (i, 0)),
                  pl.BlockSpec((tm, kr), lambda i, j: (i, 0)),
                  pl.BlockSpec((None, ka, tn), lambda i, j: (j, 0, 0)),
                  pl.BlockSpec((None, kr, tn), lambda i, j: (j, 1, 0)),
                  tile,
                  pl.BlockSpec((1, tn), lambda i, j: (0, j))],
        out_specs=[tile] + specs,
        out_shape=[jax.ShapeDtypeStruct((m, n), _F32)] + shapes,
        compiler_params=_params("parallel", "arbitrary"),
        name="out_proj",
    )(ya, yr, w_tiles, w_tiles, x, gain.reshape(1, n))


def _prefetched_row_tile(n_col, a_hbm, buf_ref, sem_ref):
    i = pl.program_id(0)
    j = pl.program_id(1)
    tm = buf_ref.shape[1]
    chunk = tm // n_col

    def chunk_copy(tile, c):
        src = a_hbm.at[pl.ds(tile * tm + c * chunk, chunk), :]
        dst = buf_ref.at[tile % 2, pl.ds(c * chunk, chunk), :]
        return pltpu.make_async_copy(src, dst, sem_ref.at[tile % 2])

    @pl.when((i == 0) & (j == 0))
    def _():
        for c in range(n_col):
            chunk_copy(0, c).start()

    @pl.when(i + 1 < pl.num_programs(0))
    def _():
        chunk_copy(i + 1, j).start()

    @pl.when(j == 0)
    def _():
        for c in range(n_col):
            chunk_copy(i, c).wait()

    return buf_ref.at[i % 2]


def _down_proj_body(n_col, a_hbm, w_ref, x_ref, gain_ref, o_ref, xg_ref, ssq_ref, buf_ref, sem_ref):
    a_ref = _prefetched_row_tile(n_col, a_hbm, buf_ref, sem_ref)
    x_new = x_ref[...] + _dot(a_ref[...], w_ref[...])
    o_ref[...] = x_new
    _emit_norm_inputs(x_new, gain_ref, xg_ref, ssq_ref, pl.program_id(1))


def _down_proj_last_body(n_col, a_hbm, w_ref, x_ref, o_ref, buf_ref, sem_ref):
    a_ref = _prefetched_row_tile(n_col, a_hbm, buf_ref, sem_ref)
    o_ref[...] = x_ref[...] + _dot(a_ref[...], w_ref[...])


def _down_proj(a, w_tiles, x, gain=None, tm=512):
    m, k = a.shape
    n_col, _, tn = w_tiles.shape
    n = n_col * tn
    tm = _fit(m, tm)
    assert (tm // n_col) * n_col == tm and (tm // n_col) % BF16_SUBLANES == 0
    tile = pl.BlockSpec((tm, tn), lambda i, j: (i, j))
    in_specs = [pl.BlockSpec(memory_space=pl.ANY), pl.BlockSpec((None, k, tn), lambda i, j: (j, 0, 0)), tile]
    scratch = [pltpu.VMEM((2, tm, k), a.dtype), pltpu.SemaphoreType.DMA((2,))]
    params = _params("arbitrary", "arbitrary")
    if gain is None:
        return pl.pallas_call(
            functools.partial(_down_proj_last_body, n_col),
            grid=(m // tm, n_col),
            in_specs=in_specs,
            out_specs=tile,
            out_shape=jax.ShapeDtypeStruct((m, n), _F32),
            scratch_shapes=scratch,
            compiler_params=params,
            name="down_proj_last",
        )(a, w_tiles, x)
    specs, shapes = _norm_inputs_out(m, n, tm, tn)
    return pl.pallas_call(
        functools.partial(_down_proj_body, n_col),
        grid=(m // tm, n_col),
        in_specs=in_specs + [pl.BlockSpec((1, tn), lambda i, j: (0, j))],
        out_specs=[tile] + specs,
        out_shape=[jax.ShapeDtypeStruct((m, n), _F32)] + shapes,
        scratch_shapes=scratch,
        compiler_params=params,
        name="down_proj",
    )(a, w_tiles, x, gain.reshape(1, n))


PROJ_TILE = 512


def _store_weight_tiles(w_ref, tiles_ref):
    tn = tiles_ref.shape[2]
    for c in range(tiles_ref.shape[0]):
        tiles_ref[c] = w_ref[:, c * tn:(c + 1) * tn].astype(tiles_ref.dtype)


def _neighbour_rows(x, prev_row, next_row):
    tm = x.shape[0]
    rows = lax.broadcasted_iota(jnp.int32, x.shape, 0)
    up = jnp.where(rows == 0, prev_row, pltpu.roll(x, 1, 0))
    dn = jnp.where(rows == tm - 1, next_row, pltpu.roll(x, tm - 1, 0))
    return up, dn


def _halo_specs(tm, tc, col_block, rows_total, halo=BF16_SUBLANES):
    per = tm // halo
    last = rows_total // halo - 1
    prev = pl.BlockSpec((halo, tc), lambda i, c: (jnp.maximum(i * per - 1, 0), col_block(c)))
    nxt = pl.BlockSpec((halo, tc), lambda i, c: (jnp.minimum((i + 1) * per, last), col_block(c)))
    return prev, nxt


def _conv_mixer_body(tiles_per_seq, b_ref, c_ref, x_ref, cp_ref, xp_ref, cn_ref, xn_ref,
                     w_ref, bias_ref, beta_ref, o_ref):
    f = lambda r: r[...].astype(_F32)
    u = f(c_ref) * f(x_ref)
    t = pl.program_id(0) % tiles_per_seq
    keep_prev = (t != 0).astype(_F32)
    keep_next = (t != tiles_per_seq - 1).astype(_F32)
    halo = cp_ref.shape[0]
    u_prev = (f(cp_ref) * f(xp_ref))[halo - 1:halo, :] * keep_prev
    u_next = (f(cn_ref) * f(xn_ref))[0:1, :] * keep_next
    up, dn = _neighbour_rows(u, u_prev, u_next)
    w = w_ref[...]
    conv = up * w[0:1, :] + u * w[1:2, :] + dn * w[2:3, :] + bias_ref[...]
    ya = f(b_ref) * conv
    tm, tc = ya.shape
    beta = beta_ref[...]
    for g in range(tc // CONV_GROUP):
        sl = slice(g * CONV_GROUP, (g + 1) * CONV_GROUP)
        yg = ya[:, sl]
        ms = jnp.mean(yg * yg, axis=-1, keepdims=True)
        o_ref[:, sl] = (yg * lax.rsqrt(ms + EPS) * beta[:, sl]).astype(o_ref.dtype)


def _conv_mixer(proj, conv_w, conv_b, beta, layer, seq, tm=1024, tc=1024):
    m = proj.shape[0]
    cw = conv_w.shape[-1]
    tm, tc = _fit(seq, tm), _fit(cw, tc)
    nb = cw // tc
    col = lambda off: (lambda c: c + off * nb)
    main = lambda off: pl.BlockSpec((tm, tc), lambda i, c: (i, c + off * nb))
    cp, cn = _halo_specs(tm, tc, col(1), m)
    xp, xn = _halo_specs(tm, tc, col(2), m)
    vec = lambda rows: pl.BlockSpec((None, rows, tc), lambda i, c: (layer, 0, c))
    depth = conv_w.shape[0]
    return pl.pallas_call(
        functools.partial(_conv_mixer_body, seq // tm),
        grid=(m // tm, nb),
        in_specs=[main(0), main(1), main(2), cp, xp, cn, xn, vec(3), vec(1), vec(1)],
        out_specs=pl.BlockSpec((tm, tc), lambda i, c: (i, c)),
        out_shape=jax.ShapeDtypeStruct((m, cw), _BF16),
        compiler_params=_params("parallel", "parallel"),
        name="conv_mixer",
    )(proj, proj, proj, proj, proj, proj, proj, conv_w, conv_b.reshape(depth, 1, cw), beta.reshape(depth, 1, cw))


def _rope_tables_body(pos_ref, inv_ref, cos_ref, sin_ref):
    ang = pos_ref[...].astype(_F32) * inv_ref[...]
    cos_ref[...] = jnp.cos(ang)
    sin_ref[...] = jnp.sin(ang)


def _rope_tables(positions, tm=1024):
    m = positions.size
    tm = _fit(m, tm)
    half = HEAD_DIM // 2
    inv_freq = ROPE_BASE ** (-jnp.arange(half, dtype=_F32) / half)
    out = jax.ShapeDtypeStruct((m, half), _F32)
    return pl.pallas_call(
        _rope_tables_body,
        grid=(m // tm,),
        in_specs=[pl.BlockSpec((tm, 1), lambda i: (i, 0)),
                  pl.BlockSpec((1, half), lambda i: (0, 0))],
        out_specs=[pl.BlockSpec((tm, half), lambda i: (i, 0))] * 2,
        out_shape=[out, out],
        compiler_params=_params("parallel"),
        name="rope_tables",
    )(positions.reshape(m, 1), inv_freq.reshape(1, half))


def _rotary(t, cos, sin):
    half = t.shape[-1] // 2
    t1, t2 = t[:, :half], t[:, half:]
    return jnp.concatenate([t1 * cos - t2 * sin, t2 * cos + t1 * sin], axis=-1)


def _retention_body(n_heads, logit_ref, q_ref, k_ref, v_ref, g_ref, cos_ref, sin_ref, gn_ref, wo_ref,
                    o_ref, wo_tiles_ref, krot_ref, kvf_ref, kvb_ref, sf_ref, sb_ref, state_ref):
    _store_weight_tiles(wo_ref, wo_tiles_ref)

    h = pl.program_id(1)
    seq, d = q_ref.shape
    c = CHUNK
    n_chunks = seq // c

    def log_gamma(shape, direction):
        return jax.nn.log_sigmoid(jnp.full(shape, logit_ref[direction * n_heads + h], _F32))

    def chunk_rows(n):
        return pl.ds(pl.multiple_of(n * c, c), c)

    pos_cd = lax.broadcasted_iota(jnp.int32, (c, d), 0).astype(_F32)
    lf = log_gamma((c, d), 0)
    lb = log_gamma((c, d), 1)

    wk_f = jnp.exp(lf * (c - 1.0 - pos_cd))
    wk_b = jnp.exp(lb * pos_cd)

    def summarise(n, carry):
        rows = chunk_rows(n)
        k = _rotary(k_ref[rows, :].astype(_F32), cos_ref[rows, :], sin_ref[rows, :])
        krot_ref[rows, :] = k.astype(_BF16)
        v = v_ref[rows, :]
        contract_rows = (((0,), (0,)), ((), ()))
        kvf_ref[n] = lax.dot_general((k * wk_f).astype(_BF16), v, contract_rows, preferred_element_type=_F32)
        kvb_ref[n] = lax.dot_general((k * wk_b).astype(_BF16), v, contract_rows, preferred_element_type=_F32)
        return carry

    lax.fori_loop(0, n_chunks, summarise, 0, unroll=4)

    def scan(summary_ref, out_ref, log_g, order):
        g_chunk = jnp.exp(log_g * float(c))
        state_ref[...] = jnp.zeros_like(state_ref)

        def step(t, carry):
            n = order(t)
            state = state_ref[...]
            out_ref[n] = state.astype(_BF16)
            state_ref[...] = state * g_chunk + summary_ref[n]
            return carry

        lax.fori_loop(0, n_chunks, step, 0, unroll=4)

    scan(kvf_ref, sf_ref, log_gamma((1, d), 0), lambda t: t)
    scan(kvb_ref, sb_ref, log_gamma((1, d), 1), lambda t: n_chunks - 1 - t)

    row = lax.broadcasted_iota(jnp.int32, (c, c), 0)
    col = lax.broadcasted_iota(jnp.int32, (c, c), 1)
    rel = (row - col).astype(_F32)
    scale = HEAD_DIM ** -0.5
    dec = jnp.where(rel >= 0,
                    jnp.exp(log_gamma((c, c), 0) * jnp.maximum(rel, 0.0)),
                    jnp.exp(log_gamma((c, c), 1) * jnp.maximum(-rel, 0.0))) * scale
    wq_f = jnp.exp(lf * (pos_cd + 1.0)) * scale
    wq_b = jnp.exp(lb * (c - pos_cd)) * scale
    gain = gn_ref[...]

    def output(n, carry):
        rows = chunk_rows(n)
        q = _rotary(q_ref[rows, :].astype(_F32), cos_ref[rows, :], sin_ref[rows, :]).astype(_BF16)
        scores = lax.dot_general(q, krot_ref[rows, :], (((1,), (1,)), ((), ())),
                                 preferred_element_type=_F32) * dec
        o = _dot(scores.astype(_BF16), v_ref[rows, :])
        o = o + wq_f * _dot(q, sf_ref[n]) + wq_b * _dot(q, sb_ref[n])
        mu = jnp.mean(o, axis=-1, keepdims=True)
        cen = o - mu
        var = jnp.mean(cen * cen, axis=-1, keepdims=True)
        normed = cen * lax.rsqrt(var + EPS) * gain
        gate = g_ref[rows, :].astype(_F32)
        o_ref[rows, :] = (normed * jax.nn.silu(gate)).astype(o_ref.dtype)
        return carry

    lax.fori_loop(0, n_chunks, output, 0, unroll=4)


def _retention(proj, cos, sin, decay_logit, gn_g, w_out, layer, batch, seq, col0):
    m = proj.shape[0]
    depth, rw = gn_g.shape
    n_heads = rw // HEAD_DIM
    _, wo_rows, wo_cols = w_out.shape
    tn = _fit(wo_cols, PROJ_TILE)
    slab = wo_rows // (batch * n_heads)
    assert slab * batch * n_heads == wo_rows and slab % BF16_SUBLANES == 0
    first = col0 // HEAD_DIM
    n_chunks = seq // CHUNK
    part = lambda p: pl.BlockSpec((seq, HEAD_DIM), lambda b, h: (b, first + p * n_heads + h))
    table = pl.BlockSpec((seq, HEAD_DIM // 2), lambda b, h: (b, 0), pipeline_mode=pl.Buffered(1))
    summaries = pltpu.VMEM((n_chunks, HEAD_DIM, HEAD_DIM), _F32)
    states = pltpu.VMEM((n_chunks, HEAD_DIM, HEAD_DIM), _BF16)
    return pl.pallas_call(
        functools.partial(_retention_body, n_heads),
        grid=(batch, n_heads),
        in_specs=[pl.BlockSpec(memory_space=pltpu.SMEM),
                  part(0), part(1), part(2), part(3), table, table,
                  pl.BlockSpec((None, 1, HEAD_DIM), lambda b, h: (layer, 0, h)),
                  pl.BlockSpec((None, slab, wo_cols), lambda b, h: (layer, b * n_heads + h, 0))],
        out_specs=[pl.BlockSpec((seq, HEAD_DIM), lambda b, h: (b, h)),
                   pl.BlockSpec((wo_cols // tn, slab, tn), lambda b, h: (0, b * n_heads + h, 0))],
        out_shape=[jax.ShapeDtypeStruct((m, rw), _BF16),
                   jax.ShapeDtypeStruct((wo_cols // tn, wo_rows, tn), _BF16)],
        scratch_shapes=[pltpu.VMEM((seq, HEAD_DIM), _BF16), summaries, summaries, states, states,
                        pltpu.VMEM((HEAD_DIM, HEAD_DIM), _F32)],
        compiler_params=_params("parallel", "parallel"),
        name="retention",
    )(decay_logit[layer].reshape(-1), proj, proj, proj, proj, cos, sin, gn_g.reshape(depth, 1, rw), w_out)


def _ffn_up_body(tiles_per_seq, n_sub, h_ref, hp_ref, hn_ref, ssq_ref, ssqp_ref, ssqn_ref,
                 wg_ref, wu_ref, cw_ref, cb_ref, wd_ref, o_ref, wd_tiles_ref):
    @pl.when(pl.program_id(0) == 0)
    def _():
        _store_weight_tiles(wd_ref, wd_tiles_ref)

    tm, d_model = h_ref.shape
    halo = hp_ref.shape[0]
    wg = wg_ref[...].astype(_BF16)
    wu = wu_ref[...].astype(_BF16)
    w = cw_ref[...]
    bias = cb_ref[...]
    t = pl.program_id(0) % tiles_per_seq
    scale_prev = _row_scale(ssqp_ref[...], d_model)[-1:, :] * (t != 0).astype(_F32)
    scale_next = _row_scale(ssqn_ref[...], d_model)[0:1, :] * (t != tiles_per_seq - 1).astype(_F32)

    sub = tm // n_sub
    rows = lambda s: slice(s * sub, (s + 1) * sub)
    scale = lambda s: _row_scale(ssq_ref[rows(s), :], d_model)

    def gate_pre(s):
        parts = ([hp_ref[...]] if s == 0 else []) + [h_ref[rows(s), :]] + ([hn_ref[...]] if s == n_sub - 1 else [])
        g = _dot(parts[0] if len(parts) == 1 else jnp.concatenate(parts, axis=0), wg)
        lo = halo if s == 0 else 0
        before = g[lo - 1:lo, :] * scale_prev if s == 0 else None
        after = g[lo + sub:lo + sub + 1, :] * scale_next if s == n_sub - 1 else None
        return g[lo:lo + sub, :] * scale(s), before, after

    g_cur, prev_row, g_next = gate_pre(0)
    for s in range(n_sub):
        if s + 1 < n_sub:
            g_after, _, g_next = gate_pre(s + 1)
            next_row = g_after[0:1, :]
        else:
            next_row = g_next
        u = _dot(h_ref[rows(s), :], wu) * scale(s)
        g_up, g_dn = _neighbour_rows(g_cur, prev_row, next_row)
        gate = g_up * w[0:1, :] + g_cur * w[1:2, :] + g_dn * w[2:3, :] + bias
        o_ref[rows(s), :] = (jax.nn.silu(gate) * u).astype(o_ref.dtype)
        prev_row = g_cur[sub - 1:sub, :]
        if s + 1 < n_sub:
            g_cur = g_after


def _ffn_up(h, ssq, w_gate, w_up, conv_w, conv_b, w_down, layer, seq, tm=2048, tn=256, sub=512):
    m, k = h.shape
    depth, _, ff = w_gate.shape
    d_out = w_down.shape[2]
    tm, tn = _fit(seq, tm), _fit(ff, tn)
    n_col = ff // tn
    dt = _fit(d_out, PROJ_TILE)
    halo = BF16_SUBLANES
    f32_rows = 8
    once = dict(pipeline_mode=pl.Buffered(1))
    before = lambda rows: (lambda i, j: (jnp.maximum(i * (tm // rows) - 1, 0), 0))
    after = lambda rows: (lambda i, j: (jnp.minimum((i + 1) * (tm // rows), m // rows - 1), 0))
    weight = pl.BlockSpec((None, k, tn), lambda i, j: (layer, 0, j))
    vec = lambda rows: pl.BlockSpec((None, rows, tn), lambda i, j: (layer, 0, j))
    return pl.pallas_call(
        functools.partial(_ffn_up_body, seq // tm, tm // _fit(tm, sub)),
        grid=(m // tm, ff // tn),
        in_specs=[pl.BlockSpec((tm, k), lambda i, j: (i, 0), **once),
                  pl.BlockSpec((halo, k), before(halo)),
                  pl.BlockSpec((halo, k), after(halo)),
                  pl.BlockSpec((tm, SSQ_LANES), lambda i, j: (i, 0), **once),
                  pl.BlockSpec((f32_rows, SSQ_LANES), before(f32_rows)),
                  pl.BlockSpec((f32_rows, SSQ_LANES), after(f32_rows)),
                  weight, weight, vec(3), vec(1),
                  pl.BlockSpec((None, tn, d_out), lambda i, j: (layer, jnp.where(i == 0, j, 0), 0))],
        out_specs=[pl.BlockSpec((tm, tn), lambda i, j: (i, j)),
                   pl.BlockSpec((d_out // dt, tn, dt), lambda i, j: (0, jnp.where(i == 0, j, n_col - 1), 0))],
        out_shape=[jax.ShapeDtypeStruct((m, ff), _BF16), jax.ShapeDtypeStruct((d_out // dt, ff, dt), _BF16)],
        compiler_params=_params("arbitrary", "arbitrary"),
        name="ffn_up",
    )(h, h, h, ssq, ssq, ssq, w_gate, w_up, conv_w, conv_b.reshape(depth, 1, ff), w_down)


def kernel(x, positions, norm1_g, w_in, conv_a_w, conv_a_b, beta_a, ret_decay_logit, ret_gn_g, w_out,
           norm2_g, w_gate, w_up, ffn_conv_w, ffn_conv_b, w_down, norm_f_g):
    batch, seq, d_model = x.shape
    depth = w_in.shape[0]
    conv_width = conv_a_w.shape[-1]
    m = batch * seq

    xf = x.reshape(m, d_model)
    cos, sin = _rope_tables(positions)
    xg, ssq = _stream_in(xf, norm1_g[0])

    for l in range(depth):
        proj = _in_proj(xg, ssq, w_in, l)
        ya = _conv_mixer(proj, conv_a_w, conv_a_b, beta_a, l, seq)
        yr, w_o = _retention(proj, cos, sin, ret_decay_logit, ret_gn_g, w_out, l, batch, seq, 3 * conv_width)
        xf, xg, ssq = _out_proj(ya, yr, w_o, xf, norm2_g[l])
        hidden, w_dn = _ffn_up(xg, ssq, w_gate, w_up, ffn_conv_w, ffn_conv_b, w_down, l, seq)
        if l + 1 < depth:
            xf, xg, ssq = _down_proj(hidden, w_dn, xf, norm1_g[l + 1])
        else:
            xf = _down_proj(hidden, w_dn, xf)

    return _rmsnorm(xf, norm_f_g, x.dtype).reshape(batch, seq, d_model)
```

```python
import functools

import jax
import jax.numpy as jnp
from jax import lax
from jax.experimental import pallas as pl
from jax.experimental.pallas import tpu as pltpu

EPS = 1e-6
ROPE_BASE = 10000.0
HEAD_DIM = 256
CONV_GROUP = 128
CHUNK = 256
BF16_SUBLANES = 16
V7X_VMEM_LIMIT = 56 * 1024 * 1024

_BF16 = jnp.bfloat16
_F32 = jnp.float32


def _params(*sem):
    return pltpu.CompilerParams(dimension_semantics=sem, vmem_limit_bytes=V7X_VMEM_LIMIT)


def _dot(a, b):
    return jnp.dot(a, b, preferred_element_type=_F32)


def _fit(dim, tile):
    return tile if dim % tile == 0 else dim


SSQ_LANES = 128


def _row_scale(ssq, width):
    return lax.rsqrt(jnp.sum(ssq, axis=-1, keepdims=True) / width + EPS)


def _emit_norm_inputs(x_new, gain_ref, xg_ref, ssq_ref, col_step):
    xg_ref[...] = (x_new * gain_ref[...]).astype(xg_ref.dtype)
    sq = x_new * x_new
    part = sq[:, 0:SSQ_LANES]
    for c in range(1, sq.shape[1] // SSQ_LANES):
        part = part + sq[:, c * SSQ_LANES:(c + 1) * SSQ_LANES]

    @pl.when(col_step == 0)
    def _():
        ssq_ref[...] = part

    @pl.when(col_step != 0)
    def _():
        ssq_ref[...] += part


def _norm_inputs_out(m, n, tm, tn):
    specs = [pl.BlockSpec((tm, tn), lambda i, j: (i, j)), pl.BlockSpec((tm, SSQ_LANES), lambda i, j: (i, 0))]
    shapes = [jax.ShapeDtypeStruct((m, n), _BF16), jax.ShapeDtypeStruct((m, SSQ_LANES), _F32)]
    return specs, shapes


def _stream_in_body(x_ref, g_ref, xg_ref, ssq_ref):
    _emit_norm_inputs(x_ref[...], g_ref, xg_ref, ssq_ref, pl.program_id(1))


def _stream_in(x, gain, tm=256):
    m, d = x.shape
    tm = _fit(m, tm)
    specs, shapes = _norm_inputs_out(m, d, tm, d)
    return pl.pallas_call(
        _stream_in_body,
        grid=(m // tm, 1),
        in_specs=[pl.BlockSpec((tm, d), lambda i, j: (i, 0)),
                  pl.BlockSpec((1, d), lambda i, j: (0, 0))],
        out_specs=specs,
        out_shape=shapes,
        compiler_params=_params("parallel", "arbitrary"),
        name="stream_in",
    )(x, gain.reshape(1, d))


def _rmsnorm_body(x_ref, g_ref, o_ref):
    x = x_ref[...]
    ms = jnp.mean(x * x, axis=-1, keepdims=True)
    o_ref[...] = (x * lax.rsqrt(ms + EPS) * g_ref[...]).astype(o_ref.dtype)


def _rmsnorm(x, g, out_dtype, tm=256):
    m, d = x.shape
    tm = _fit(m, tm)
    return pl.pallas_call(
        _rmsnorm_body,
        grid=(m // tm,),
        in_specs=[pl.BlockSpec((tm, d), lambda i: (i, 0)),
                  pl.BlockSpec((1, d), lambda i: (0, 0))],
        out_specs=pl.BlockSpec((tm, d), lambda i: (i, 0)),
        out_shape=jax.ShapeDtypeStruct((m, d), out_dtype),
        compiler_params=_params("parallel"),
        name="rmsnorm",
    )(x, g.reshape(1, d))


def _in_proj_body(a_ref, ssq_ref, wslice_ref, o_ref, w_bf16_ref):
    j = pl.program_id(0)
    i = pl.program_id(1)
    rows = wslice_ref.shape[0]

    def round_slice():
        dst = pl.ds(pl.multiple_of(i * rows, rows), rows)
        w_bf16_ref[j % 2, dst, :] = wslice_ref[...].astype(w_bf16_ref.dtype)

    @pl.when(j == 0)
    def _():
        round_slice()

    @pl.when(j > 0)
    def _():
        scale = _row_scale(ssq_ref[...], a_ref.shape[1])
        o_ref[...] = (_dot(a_ref[...], w_bf16_ref[(j - 1) % 2]) * scale).astype(o_ref.dtype)
        round_slice()


def _in_proj(a, ssq, w, layer, tm=1024, tn=1024):
    m, k = a.shape
    n = w.shape[2]
    tm, tn = _fit(m, tm), _fit(n, tn)
    n_row, n_col = m // tm, n // tn
    rows = k // n_row
    assert rows * n_row == k and rows % BF16_SUBLANES == 0
    row_tile = lambda j, i: jnp.where(j > 0, i, 0)
    return pl.pallas_call(
        _in_proj_body,
        grid=(n_col + 1, n_row),
        in_specs=[pl.BlockSpec((tm, k), lambda j, i: (row_tile(j, i), 0)),
                  pl.BlockSpec((tm, SSQ_LANES), lambda j, i: (row_tile(j, i), 0)),
                  pl.BlockSpec((None, rows, tn), lambda j, i: (layer, i, jnp.minimum(j, n_col - 1)))],
        out_specs=pl.BlockSpec((tm, tn), lambda j, i: (row_tile(j, i), jnp.maximum(j - 1, 0))),
        out_shape=jax.ShapeDtypeStruct((m, n), _BF16),
        scratch_shapes=[pltpu.VMEM((2, k, tn), _BF16)],
        compiler_params=_params("arbitrary", "arbitrary"),
        name="in_proj",
    )(a, ssq, w)


def _prefetched_row_tile(n_col, a_hbm, buf_ref, sem_ref):
    i = pl.program_id(0)
    j = pl.program_id(1)
    tm = buf_ref.shape[1]
    chunk = tm // n_col

    def chunk_copy(tile, c):
        src = a_hbm.at[pl.ds(tile * tm + c * chunk, chunk), :]
        dst = buf_ref.at[tile % 2, pl.ds(c * chunk, chunk), :]
        return pltpu.make_async_copy(src, dst, sem_ref.at[tile % 2])

    @pl.when((i == 0) & (j == 0))
    def _():
        for c in range(n_col):
            chunk_copy(0, c).start()

    @pl.when(i + 1 < pl.num_programs(0))
    def _():
        chunk_copy(i + 1, j).start()

    @pl.when(j == 0)
    def _():
        for c in range(n_col):
            chunk_copy(i, c).wait()

    return buf_ref.at[i % 2]


def _down_proj_body(n_col, a_hbm, w_ref, x_ref, gain_ref, o_ref, xg_ref, ssq_ref, buf_ref, sem_ref):
    a_ref = _prefetched_row_tile(n_col, a_hbm, buf_ref, sem_ref)
    x_new = x_ref[...] + _dot(a_ref[...], w_ref[...])
    o_ref[...] = x_new
    _emit_norm_inputs(x_new, gain_ref, xg_ref, ssq_ref, pl.program_id(1))


def _down_proj_last_body(n_col, a_hbm, w_ref, x_ref, o_ref, buf_ref, sem_ref):
    a_ref = _prefetched_row_tile(n_col, a_hbm, buf_ref, sem_ref)
    o_ref[...] = x_ref[...] + _dot(a_ref[...], w_ref[...])


def _down_proj(a, w_tiles, x, gain=None, tm=512):
    m, k = a.shape
    n_col, _, tn = w_tiles.shape
    n = n_col * tn
    tm = _fit(m, tm)
    assert (tm // n_col) * n_col == tm and (tm // n_col) % BF16_SUBLANES == 0
    tile = pl.BlockSpec((tm, tn), lambda i, j: (i, j))
    in_specs = [pl.BlockSpec(memory_space=pl.ANY), pl.BlockSpec((None, k, tn), lambda i, j: (j, 0, 0)), tile]
    scratch = [pltpu.VMEM((2, tm, k), a.dtype), pltpu.SemaphoreType.DMA((2,))]
    params = _params("arbitrary", "arbitrary")
    if gain is None:
        return pl.pallas_call(
            functools.partial(_down_proj_last_body, n_col),
            grid=(m // tm, n_col),
            in_specs=in_specs,
            out_specs=tile,
            out_shape=jax.ShapeDtypeStruct((m, n), _F32),
            scratch_shapes=scratch,
            compiler_params=params,
            name="down_proj_last",
        )(a, w_tiles, x)
    specs, shapes = _norm_inputs_out(m, n, tm, tn)
    return pl.pallas_call(
        functools.partial(_down_proj_body, n_col),
        grid=(m // tm, n_col),
        in_specs=in_specs + [pl.BlockSpec((1, tn), lambda i, j: (0, j))],
        out_specs=[tile] + specs,
        out_shape=[jax.ShapeDtypeStruct((m, n), _F32)] + shapes,
        scratch_shapes=scratch,
        compiler_params=params,
        name="down_proj",
    )(a, w_tiles, x, gain.reshape(1, n))


PROJ_TILE = 512


def _store_weight_tiles(w_ref, tiles_ref):
    tn = tiles_ref.shape[2]
    for c in range(tiles_ref.shape[0]):
        tiles_ref[c] = w_ref[:, c * tn:(c + 1) * tn].astype(tiles_ref.dtype)


def _neighbour_rows(x, prev_row, next_row):
    tm = x.shape[0]
    rows = lax.broadcasted_iota(jnp.int32, x.shape, 0)
    up = jnp.where(rows == 0, prev_row, pltpu.roll(x, 1, 0))
    dn = jnp.where(rows == tm - 1, next_row, pltpu.roll(x, tm - 1, 0))
    return up, dn


def _conv_mixer_tile(tile_in_seq, tiles_per_seq, b_ref, c_ref, x_ref, cp_ref, xp_ref, cn_ref, xn_ref,
                     w_ref, bias_ref, beta_ref):
    f = lambda r: r[...].astype(_F32)
    u = f(c_ref) * f(x_ref)
    keep_prev = (tile_in_seq != 0).astype(_F32)
    keep_next = (tile_in_seq != tiles_per_seq - 1).astype(_F32)
    halo = cp_ref.shape[0]
    u_prev = (f(cp_ref) * f(xp_ref))[halo - 1:halo, :] * keep_prev
    u_next = (f(cn_ref) * f(xn_ref))[0:1, :] * keep_next
    up, dn = _neighbour_rows(u, u_prev, u_next)
    w = w_ref[...]
    conv = up * w[0:1, :] + u * w[1:2, :] + dn * w[2:3, :] + bias_ref[...]
    ya = f(b_ref) * conv
    beta = beta_ref[...]
    groups = []
    for g in range(ya.shape[1] // CONV_GROUP):
        sl = slice(g * CONV_GROUP, (g + 1) * CONV_GROUP)
        yg = ya[:, sl]
        ms = jnp.mean(yg * yg, axis=-1, keepdims=True)
        groups.append(yg * lax.rsqrt(ms + EPS) * beta[:, sl])
    return jnp.concatenate(groups, axis=1)


def _out_proj_body(tiles_per_seq, b_ref, c_ref, xc_ref, cp_ref, xp_ref, cn_ref, xn_ref, cw_ref, cb_ref, beta_ref,
                   yr_ref, wa_ref, wr_ref, x_ref, gain_ref, o_ref, xg_ref, ssq_ref, ya_ref):
    r = pl.program_id(0)
    j = pl.program_id(1)
    n_row = pl.num_programs(0) - 1
    cw = b_ref.shape[1]

    def mix():
        ya = _conv_mixer_tile(r % tiles_per_seq, tiles_per_seq, b_ref, c_ref, xc_ref, cp_ref, xp_ref, cn_ref, xn_ref,
                              cw_ref, cb_ref, beta_ref)
        ya_ref[r % 2, :, pl.ds(pl.multiple_of(j * cw, cw), cw)] = ya.astype(ya_ref.dtype)

    def project():
        y = _dot(ya_ref[(r - 1) % 2], wa_ref[...]) + _dot(yr_ref[...], wr_ref[...])
        x_new = x_ref[...] + y
        o_ref[...] = x_new
        _emit_norm_inputs(x_new, gain_ref, xg_ref, ssq_ref, j)

    @pl.when(r == 0)
    def _():
        mix()

    @pl.when((r > 0) & (r < n_row))
    def _():
        mix()
        project()

    @pl.when(r == n_row)
    def _():
        project()


def _out_proj(proj, yr, w_tiles, conv_w, conv_b, beta, layer, x, gain, seq, tm=1024):
    m = proj.shape[0]
    depth, _, ka = conv_w.shape
    kr = yr.shape[1]
    n_col, _, tn = w_tiles.shape
    n = n_col * tn
    assert ka == kr
    tm = _fit(seq, tm)
    n_row = m // tm
    cw = ka // n_col
    assert cw * n_col == ka and cw % CONV_GROUP == 0
    nb = ka // cw
    halo = BF16_SUBLANES
    per = tm // halo
    mix_tile = lambda r: jnp.minimum(r, n_row - 1)
    out_tile = lambda r: jnp.maximum(r - 1, 0)
    main = lambda off: pl.BlockSpec((tm, cw), lambda r, j: (mix_tile(r), off * nb + j))
    before = lambda off: pl.BlockSpec((halo, cw), lambda r, j: (jnp.maximum(mix_tile(r) * per - 1, 0), off * nb + j))
    after = lambda off: pl.BlockSpec(
        (halo, cw), lambda r, j: (jnp.minimum((mix_tile(r) + 1) * per, m // halo - 1), off * nb + j))
    vec = lambda rows: pl.BlockSpec((None, rows, cw), lambda r, j: (layer, 0, j))
    tile = pl.BlockSpec((tm, tn), lambda r, j: (out_tile(r), jnp.where(r == 0, 0, j)))
    return pl.pallas_call(
        functools.partial(_out_proj_body, seq // tm),
        grid=(n_row + 1, n_col),
        in_specs=[main(0), main(1), main(2), before(1), before(2), after(1), after(2), vec(3), vec(1), vec(1),
                  pl.BlockSpec((tm, kr), lambda r, j: (out_tile(r), 0)),
                  pl.BlockSpec((None, ka, tn), lambda r, j: (j, 0, 0)),
                  pl.BlockSpec((None, kr, tn), lambda r, j: (j, 1, 0)),
                  tile,
                  pl.BlockSpec((1, tn), lambda r, j: (0, j))],
        out_specs=[tile, tile, pl.BlockSpec((tm, SSQ_LANES), lambda r, j: (out_tile(r), 0))],
        out_shape=[jax.ShapeDtypeStruct((m, n), _F32), jax.ShapeDtypeStruct((m, n), _BF16),
                   jax.ShapeDtypeStruct((m, SSQ_LANES), _F32)],
        scratch_shapes=[pltpu.VMEM((2, tm, ka), _BF16)],
        compiler_params=_params("arbitrary", "arbitrary"),
        name="out_proj",
    )(proj, proj, proj, proj, proj, proj, proj, conv_w, conv_b.reshape(depth, 1, ka), beta.reshape(depth, 1, ka),
      yr, w_tiles, w_tiles, x, gain.reshape(1, n))


def _rope_tables_body(pos_ref, inv_ref, cos_ref, sin_ref):
    ang = pos_ref[...].astype(_F32) * inv_ref[...]
    cos_ref[...] = jnp.cos(ang)
    sin_ref[...] = jnp.sin(ang)


def _rope_tables(positions, tm=1024):
    m = positions.size
    tm = _fit(m, tm)
    half = HEAD_DIM // 2
    inv_freq = ROPE_BASE ** (-jnp.arange(half, dtype=_F32) / half)
    out = jax.ShapeDtypeStruct((m, half), _F32)
    return pl.pallas_call(
        _rope_tables_body,
        grid=(m // tm,),
        in_specs=[pl.BlockSpec((tm, 1), lambda i: (i, 0)),
                  pl.BlockSpec((1, half), lambda i: (0, 0))],
        out_specs=[pl.BlockSpec((tm, half), lambda i: (i, 0))] * 2,
        out_shape=[out, out],
        compiler_params=_params("parallel"),
        name="rope_tables",
    )(positions.reshape(m, 1), inv_freq.reshape(1, half))


def _rotary(t, cos, sin):
    half = t.shape[-1] // 2
    t1, t2 = t[:, :half], t[:, half:]
    return jnp.concatenate([t1 * cos - t2 * sin, t2 * cos + t1 * sin], axis=-1)


def _retention_body(n_heads, logit_ref, q_ref, k_ref, v_ref, g_ref, cos_ref, sin_ref, gn_ref, wo_ref,
                    o_ref, wo_tiles_ref, krot_ref, kvf_ref, kvb_ref, sf_ref, sb_ref, state_ref):
    _store_weight_tiles(wo_ref, wo_tiles_ref)

    h = pl.program_id(1)
    seq, d = q_ref.shape
    c = CHUNK
    n_chunks = seq // c

    def log_gamma(shape, direction):
        return jax.nn.log_sigmoid(jnp.full(shape, logit_ref[direction * n_heads + h], _F32))

    def chunk_rows(n):
        return pl.ds(pl.multiple_of(n * c, c), c)

    pos_cd = lax.broadcasted_iota(jnp.int32, (c, d), 0).astype(_F32)
    lf = log_gamma((c, d), 0)
    lb = log_gamma((c, d), 1)

    wk_f = jnp.exp(lf * (c - 1.0 - pos_cd))
    wk_b = jnp.exp(lb * pos_cd)

    def summarise(n, carry):
        rows = chunk_rows(n)
        k = _rotary(k_ref[rows, :].astype(_F32), cos_ref[rows, :], sin_ref[rows, :])
        krot_ref[rows, :] = k.astype(_BF16)
        v = v_ref[rows, :]
        contract_rows = (((0,), (0,)), ((), ()))
        kvf_ref[n] = lax.dot_general((k * wk_f).astype(_BF16), v, contract_rows, preferred_element_type=_F32)
        kvb_ref[n] = lax.dot_general((k * wk_b).astype(_BF16), v, contract_rows, preferred_element_type=_F32)
        return carry

    lax.fori_loop(0, n_chunks, summarise, 0, unroll=4)

    def scan(summary_ref, out_ref, log_g, order):
        g_chunk = jnp.exp(log_g * float(c))
        state_ref[...] = jnp.zeros_like(state_ref)

        def step(t, carry):
            n = order(t)
            state = state_ref[...]
            out_ref[n] = state.astype(_BF16)
            state_ref[...] = state * g_chunk + summary_ref[n]
            return carry

        lax.fori_loop(0, n_chunks, step, 0, unroll=4)

    scan(kvf_ref, sf_ref, log_gamma((1, d), 0), lambda t: t)
    scan(kvb_ref, sb_ref, log_gamma((1, d), 1), lambda t: n_chunks - 1 - t)

    row = lax.broadcasted_iota(jnp.int32, (c, c), 0)
    col = lax.broadcasted_iota(jnp.int32, (c, c), 1)
    rel = (row - col).astype(_F32)
    scale = HEAD_DIM ** -0.5
    dec = jnp.where(rel >= 0,
                    jnp.exp(log_gamma((c, c), 0) * jnp.maximum(rel, 0.0)),
                    jnp.exp(log_gamma((c, c), 1) * jnp.maximum(-rel, 0.0))) * scale
    wq_f = jnp.exp(lf * (pos_cd + 1.0)) * scale
    wq_b = jnp.exp(lb * (c - pos_cd)) * scale
    gain = gn_ref[...]

    def output(n, carry):
        rows = chunk_rows(n)
        q = _rotary(q_ref[rows, :].astype(_F32), cos_ref[rows, :], sin_ref[rows, :]).astype(_BF16)
        scores = lax.dot_general(q, krot_ref[rows, :], (((1,), (1,)), ((), ())),
                                 preferred_element_type=_F32) * dec
        o = _dot(scores.astype(_BF16), v_ref[rows, :])
        o = o + wq_f * _dot(q, sf_ref[n]) + wq_b * _dot(q, sb_ref[n])
        mu = jnp.mean(o, axis=-1, keepdims=True)
        cen = o - mu
        var = jnp.mean(cen * cen, axis=-1, keepdims=True)
        normed = cen * lax.rsqrt(var + EPS) * gain
        gate = g_ref[rows, :].astype(_F32)
        o_ref[rows, :] = (normed * jax.nn.silu(gate)).astype(o_ref.dtype)
        return carry

    lax.fori_loop(0, n_chunks, output, 0, unroll=4)


def _retention(proj, cos, sin, decay_logit, gn_g, w_out, layer, batch, seq, col0):
    m = proj.shape[0]
    depth, rw = gn_g.shape
    n_heads = rw // HEAD_DIM
    _, wo_rows, wo_cols = w_out.shape
    tn = _fit(wo_cols, PROJ_TILE)
    slab = wo_rows // (batch * n_heads)
    assert slab * batch * n_heads == wo_rows and slab % BF16_SUBLANES == 0
    first = col0 // HEAD_DIM
    n_chunks = seq // CHUNK
    part = lambda p: pl.BlockSpec((seq, HEAD_DIM), lambda b, h: (b, first + p * n_heads + h))
    table = pl.BlockSpec((seq, HEAD_DIM // 2), lambda b, h: (b, 0), pipeline_mode=pl.Buffered(1))
    summaries = pltpu.VMEM((n_chunks, HEAD_DIM, HEAD_DIM), _F32)
    states = pltpu.VMEM((n_chunks, HEAD_DIM, HEAD_DIM), _BF16)
    return pl.pallas_call(
        functools.partial(_retention_body, n_heads),
        grid=(batch, n_heads),
        in_specs=[pl.BlockSpec(memory_space=pltpu.SMEM),
                  part(0), part(1), part(2), part(3), table, table,
                  pl.BlockSpec((None, 1, HEAD_DIM), lambda b, h: (layer, 0, h)),
                  pl.BlockSpec((None, slab, wo_cols), lambda b, h: (layer, b * n_heads + h, 0))],
        out_specs=[pl.BlockSpec((seq, HEAD_DIM), lambda b, h: (b, h)),
                   pl.BlockSpec((wo_cols // tn, slab, tn), lambda b, h: (0, b * n_heads + h, 0))],
        out_shape=[jax.ShapeDtypeStruct((m, rw), _BF16),
                   jax.ShapeDtypeStruct((wo_cols // tn, wo_rows, tn), _BF16)],
        scratch_shapes=[pltpu.VMEM((seq, HEAD_DIM), _BF16), summaries, summaries, states, states,
                        pltpu.VMEM((HEAD_DIM, HEAD_DIM), _F32)],
        compiler_params=_params("parallel", "parallel"),
        name="retention",
    )(decay_logit[layer].reshape(-1), proj, proj, proj, proj, cos, sin, gn_g.reshape(depth, 1, rw), w_out)


def _ffn_up_body(tiles_per_seq, n_sub, h_ref, hp_ref, hn_ref, ssq_ref, ssqp_ref, ssqn_ref,
                 wg_ref, wu_ref, cw_ref, cb_ref, wd_ref, o_ref, wd_tiles_ref):
    @pl.when(pl.program_id(0) == 0)
    def _():
        _store_weight_tiles(wd_ref, wd_tiles_ref)

    tm, d_model = h_ref.shape
    halo = hp_ref.shape[0]
    wg = wg_ref[...].astype(_BF16)
    wu = wu_ref[...].astype(_BF16)
    w = cw_ref[...]
    bias = cb_ref[...]
    t = pl.program_id(0) % tiles_per_seq
    scale_prev = _row_scale(ssqp_ref[...], d_model)[-1:, :] * (t != 0).astype(_F32)
    scale_next = _row_scale(ssqn_ref[...], d_model)[0:1, :] * (t != tiles_per_seq - 1).astype(_F32)

    sub = tm // n_sub
    rows = lambda s: slice(s * sub, (s + 1) * sub)
    scale = lambda s: _row_scale(ssq_ref[rows(s), :], d_model)

    def gate_pre(s):
        parts = ([hp_ref[...]] if s == 0 else []) + [h_ref[rows(s), :]] + ([hn_ref[...]] if s == n_sub - 1 else [])
        g = _dot(parts[0] if len(parts) == 1 else jnp.concatenate(parts, axis=0), wg)
        lo = halo if s == 0 else 0
        before = g[lo - 1:lo, :] * scale_prev if s == 0 else None
        after = g[lo + sub:lo + sub + 1, :] * scale_next if s == n_sub - 1 else None
        return g[lo:lo + sub, :] * scale(s), before, after

    order = list(range(1, n_sub - 1)) + [0, n_sub - 1]
    gs, edge = {}, {}
    for s in order:
        gs[s], before, after = gate_pre(s)
        if before is not None:
            edge["prev"] = before
        if after is not None:
            edge["next"] = after
    for s in range(n_sub):
        u = _dot(h_ref[rows(s), :], wu) * scale(s)
        prev_row = edge["prev"] if s == 0 else gs[s - 1][sub - 1:sub, :]
        next_row = edge["next"] if s == n_sub - 1 else gs[s + 1][0:1, :]
        g_up, g_dn = _neighbour_rows(gs[s], prev_row, next_row)
        gate = g_up * w[0:1, :] + gs[s] * w[1:2, :] + g_dn * w[2:3, :] + bias
        o_ref[rows(s), :] = (jax.nn.silu(gate) * u).astype(o_ref.dtype)


def _ffn_up(h, ssq, w_gate, w_up, conv_w, conv_b, w_down, layer, seq, tm=2048, tn=256, sub=512):
    m, k = h.shape
    depth, _, ff = w_gate.shape
    d_out = w_down.shape[2]
    tm, tn = _fit(seq, tm), _fit(ff, tn)
    n_col = ff // tn
    dt = _fit(d_out, PROJ_TILE)
    halo = BF16_SUBLANES
    f32_rows = 8
    once = dict(pipeline_mode=pl.Buffered(1))
    before = lambda rows: (lambda i, j: (jnp.maximum(i * (tm // rows) - 1, 0), 0))
    after = lambda rows: (lambda i, j: (jnp.minimum((i + 1) * (tm // rows), m // rows - 1), 0))
    weight = pl.BlockSpec((None, k, tn), lambda i, j: (layer, 0, j))
    vec = lambda rows: pl.BlockSpec((None, rows, tn), lambda i, j: (layer, 0, j))
    return pl.pallas_call(
        functools.partial(_ffn_up_body, seq // tm, tm // _fit(tm, sub)),
        grid=(m // tm, ff // tn),
        in_specs=[pl.BlockSpec((tm, k), lambda i, j: (i, 0), **once),
                  pl.BlockSpec((halo, k), before(halo)),
                  pl.BlockSpec((halo, k), after(halo)),
                  pl.BlockSpec((tm, SSQ_LANES), lambda i, j: (i, 0), **once),
                  pl.BlockSpec((f32_rows, SSQ_LANES), before(f32_rows)),
                  pl.BlockSpec((f32_rows, SSQ_LANES), after(f32_rows)),
                  weight, weight, vec(3), vec(1),
                  pl.BlockSpec((None, tn, d_out), lambda i, j: (layer, jnp.where(i == 0, j, 0), 0))],
        out_specs=[pl.BlockSpec((tm, tn), lambda i, j: (i, j)),
                   pl.BlockSpec((d_out // dt, tn, dt), lambda i, j: (0, jnp.where(i == 0, j, n_col - 1), 0))],
        out_shape=[jax.ShapeDtypeStruct((m, ff), _BF16), jax.ShapeDtypeStruct((d_out // dt, ff, dt), _BF16)],
        compiler_params=_params("arbitrary", "arbitrary"),
        name="ffn_up",
    )(h, h, h, ssq, ssq, ssq, w_gate, w_up, conv_w, conv_b.reshape(depth, 1, ff), w_down)


def kernel(x, positions, norm1_g, w_in, conv_a_w, conv_a_b, beta_a, ret_decay_logit, ret_gn_g, w_out,
           norm2_g, w_gate, w_up, ffn_conv_w, ffn_conv_b, w_down, norm_f_g):
    batch, seq, d_model = x.shape
    depth = w_in.shape[0]
    conv_width = conv_a_w.shape[-1]
    m = batch * seq

    xf = x.reshape(m, d_model)
    cos, sin = _rope_tables(positions)
    xg, ssq = _stream_in(xf, norm1_g[0])

    for l in range(depth):
        proj = _in_proj(xg, ssq, w_in, l)
        yr, w_o = _retention(proj, cos, sin, ret_decay_logit, ret_gn_g, w_out, l, batch, seq, 3 * conv_width)
        xf, xg, ssq = _out_proj(proj, yr, w_o, conv_a_w, conv_a_b, beta_a, l, xf, norm2_g[l], seq)
        hidden, w_dn = _ffn_up(xg, ssq, w_gate, w_up, ffn_conv_w, ffn_conv_b, w_down, l, seq)
        if l + 1 < depth:
            xf, xg, ssq = _down_proj(hidden, w_dn, xf, norm1_g[l + 1])
        else:
            xf = _down_proj(hidden, w_dn, xf)

    return _rmsnorm(xf, norm_f_g, x.dtype).reshape(batch, seq, d_model)
```

```python
import functools

import jax
import jax.numpy as jnp
from jax import lax
from jax.experimental import pallas as pl
from jax.experimental.pallas import tpu as pltpu

EPS = 1e-6
ROPE_BASE = 10000.0
HEAD_DIM = 256
CONV_GROUP = 128
CHUNK = 256
BF16_SUBLANES = 16
V7X_VMEM_LIMIT = 56 * 1024 * 1024

_BF16 = jnp.bfloat16
_F32 = jnp.float32


def _params(*sem):
    return pltpu.CompilerParams(dimension_semantics=sem, vmem_limit_bytes=V7X_VMEM_LIMIT)


def _dot(a, b):
    return jnp.dot(a, b, preferred_element_type=_F32)


def _fit(dim, tile):
    return tile if dim % tile == 0 else dim


SSQ_LANES = 128


def _row_scale(ssq, width):
    return lax.rsqrt(jnp.sum(ssq, axis=-1, keepdims=True) / width + EPS)


def _emit_norm_inputs(x_new, gain_ref, xg_ref, ssq_ref, col_step):
    xg_ref[...] = (x_new * gain_ref[...]).astype(xg_ref.dtype)
    sq = x_new * x_new
    part = sq[:, 0:SSQ_LANES]
    for c in range(1, sq.shape[1] // SSQ_LANES):
        part = part + sq[:, c * SSQ_LANES:(c + 1) * SSQ_LANES]

    @pl.when(col_step == 0)
    def _():
        ssq_ref[...] = part

    @pl.when(col_step != 0)
    def _():
        ssq_ref[...] += part


def _norm_inputs_out(m, n, tm, tn):
    specs = [pl.BlockSpec((tm, tn), lambda i, j: (i, j)), pl.BlockSpec((tm, SSQ_LANES), lambda i, j: (i, 0))]
    shapes = [jax.ShapeDtypeStruct((m, n), _BF16), jax.ShapeDtypeStruct((m, SSQ_LANES), _F32)]
    return specs, shapes


def _stream_in_body(x_ref, g_ref, xg_ref, ssq_ref):
    _emit_norm_inputs(x_ref[...], g_ref, xg_ref, ssq_ref, pl.program_id(1))


def _stream_in(x, gain, tm=256):
    m, d = x.shape
    tm = _fit(m, tm)
    specs, shapes = _norm_inputs_out(m, d, tm, d)
    return pl.pallas_call(
        _stream_in_body,
        grid=(m // tm, 1),
        in_specs=[pl.BlockSpec((tm, d), lambda i, j: (i, 0)),
                  pl.BlockSpec((1, d), lambda i, j: (0, 0))],
        out_specs=specs,
        out_shape=shapes,
        compiler_params=_params("parallel", "arbitrary"),
        name="stream_in",
    )(x, gain.reshape(1, d))


def _rmsnorm_body(x_ref, g_ref, o_ref):
    x = x_ref[...]
    ms = jnp.mean(x * x, axis=-1, keepdims=True)
    o_ref[...] = (x * lax.rsqrt(ms + EPS) * g_ref[...]).astype(o_ref.dtype)


def _rmsnorm(x, g, out_dtype, tm=256):
    m, d = x.shape
    tm = _fit(m, tm)
    return pl.pallas_call(
        _rmsnorm_body,
        grid=(m // tm,),
        in_specs=[pl.BlockSpec((tm, d), lambda i: (i, 0)),
                  pl.BlockSpec((1, d), lambda i: (0, 0))],
        out_specs=pl.BlockSpec((tm, d), lambda i: (i, 0)),
        out_shape=jax.ShapeDtypeStruct((m, d), out_dtype),
        compiler_params=_params("parallel"),
        name="rmsnorm",
    )(x, g.reshape(1, d))


def _in_proj_body(a_ref, ssq_ref, wslice_ref, o_ref, w_bf16_ref):
    j = pl.program_id(0)
    i = pl.program_id(1)
    rows = wslice_ref.shape[0]

    def round_slice():
        dst = pl.ds(pl.multiple_of(i * rows, rows), rows)
        w_bf16_ref[j % 2, dst, :] = wslice_ref[...].astype(w_bf16_ref.dtype)

    @pl.when(j == 0)
    def _():
        round_slice()

    @pl.when(j > 0)
    def _():
        scale = _row_scale(ssq_ref[...], a_ref.shape[1])
        o_ref[...] = (_dot(a_ref[...], w_bf16_ref[(j - 1) % 2]) * scale).astype(o_ref.dtype)
        round_slice()


def _in_proj(a, ssq, w, layer, tm=1024, tn=1024):
    m, k = a.shape
    n = w.shape[2]
    tm, tn = _fit(m, tm), _fit(n, tn)
    n_row, n_col = m // tm, n // tn
    rows = k // n_row
    assert rows * n_row == k and rows % BF16_SUBLANES == 0
    row_tile = lambda j, i: jnp.where(j > 0, i, 0)
    return pl.pallas_call(
        _in_proj_body,
        grid=(n_col + 1, n_row),
        in_specs=[pl.BlockSpec((tm, k), lambda j, i: (row_tile(j, i), 0)),
                  pl.BlockSpec((tm, SSQ_LANES), lambda j, i: (row_tile(j, i), 0)),
                  pl.BlockSpec((None, rows, tn), lambda j, i: (layer, i, jnp.minimum(j, n_col - 1)))],
        out_specs=pl.BlockSpec((tm, tn), lambda j, i: (row_tile(j, i), jnp.maximum(j - 1, 0))),
        out_shape=jax.ShapeDtypeStruct((m, n), _BF16),
        scratch_shapes=[pltpu.VMEM((2, k, tn), _BF16)],
        compiler_params=_params("arbitrary", "arbitrary"),
        name="in_proj",
    )(a, ssq, w)


def _prefetched_row_tile(n_col, a_hbm, buf_ref, sem_ref, warm_up=False):
    j = pl.program_id(1)
    tile = pl.program_id(0) - (1 if warm_up else 0)
    n_tiles = pl.num_programs(0) - (1 if warm_up else 0)
    tm = buf_ref.shape[1]
    chunk = tm // n_col

    def chunk_copy(t, c):
        src = a_hbm.at[pl.ds(t * tm + c * chunk, chunk), :]
        dst = buf_ref.at[t % 2, pl.ds(c * chunk, chunk), :]
        return pltpu.make_async_copy(src, dst, sem_ref.at[t % 2])

    if not warm_up:
        @pl.when((tile == 0) & (j == 0))
        def _():
            for c in range(n_col):
                chunk_copy(0, c).start()

    @pl.when(tile + 1 < n_tiles)
    def _():
        chunk_copy(tile + 1, j).start()

    @pl.when((j == 0) & (tile >= 0))
    def _():
        for c in range(n_col):
            chunk_copy(tile, c).wait()

    return buf_ref.at[tile % 2]


def _down_proj_body(n_col, a_hbm, w_ref, x_ref, gain_ref, o_ref, xg_ref, ssq_ref, buf_ref, sem_ref):
    a_ref = _prefetched_row_tile(n_col, a_hbm, buf_ref, sem_ref)
    x_new = x_ref[...] + _dot(a_ref[...], w_ref[...])
    o_ref[...] = x_new
    _emit_norm_inputs(x_new, gain_ref, xg_ref, ssq_ref, pl.program_id(1))


def _down_proj_last_body(n_col, a_hbm, w_ref, x_ref, o_ref, buf_ref, sem_ref):
    a_ref = _prefetched_row_tile(n_col, a_hbm, buf_ref, sem_ref)
    o_ref[...] = x_ref[...] + _dot(a_ref[...], w_ref[...])


def _down_proj(a, w_tiles, x, gain=None, tm=512):
    m, k = a.shape
    n_col, _, tn = w_tiles.shape
    n = n_col * tn
    tm = _fit(m, tm)
    assert (tm // n_col) * n_col == tm and (tm // n_col) % BF16_SUBLANES == 0
    tile = pl.BlockSpec((tm, tn), lambda i, j: (i, j))
    in_specs = [pl.BlockSpec(memory_space=pl.ANY), pl.BlockSpec((None, k, tn), lambda i, j: (j, 0, 0)), tile]
    scratch = [pltpu.VMEM((2, tm, k), a.dtype), pltpu.SemaphoreType.DMA((2,))]
    params = _params("arbitrary", "arbitrary")
    if gain is None:
        return pl.pallas_call(
            functools.partial(_down_proj_last_body, n_col),
            grid=(m // tm, n_col),
            in_specs=in_specs,
            out_specs=tile,
            out_shape=jax.ShapeDtypeStruct((m, n), _F32),
            scratch_shapes=scratch,
            compiler_params=params,
            name="down_proj_last",
        )(a, w_tiles, x)
    specs, shapes = _norm_inputs_out(m, n, tm, tn)
    return pl.pallas_call(
        functools.partial(_down_proj_body, n_col),
        grid=(m // tm, n_col),
        in_specs=in_specs + [pl.BlockSpec((1, tn), lambda i, j: (0, j))],
        out_specs=[tile] + specs,
        out_shape=[jax.ShapeDtypeStruct((m, n), _F32)] + shapes,
        scratch_shapes=scratch,
        compiler_params=params,
        name="down_proj",
    )(a, w_tiles, x, gain.reshape(1, n))


PROJ_TILE = 512


def _store_weight_tiles(w_ref, tiles_ref):
    tn = tiles_ref.shape[2]
    for c in range(tiles_ref.shape[0]):
        tiles_ref[c] = w_ref[:, c * tn:(c + 1) * tn].astype(tiles_ref.dtype)


def _neighbour_rows(x, prev_row, next_row):
    tm = x.shape[0]
    rows = lax.broadcasted_iota(jnp.int32, x.shape, 0)
    up = jnp.where(rows == 0, prev_row, pltpu.roll(x, 1, 0))
    dn = jnp.where(rows == tm - 1, next_row, pltpu.roll(x, tm - 1, 0))
    return up, dn


def _conv_mixer_tile(tile_in_seq, tiles_per_seq, b_ref, c_ref, x_ref, cp_ref, xp_ref, cn_ref, xn_ref,
                     w_ref, bias_ref, beta_ref):
    f = lambda r: r[...].astype(_F32)
    u = f(c_ref) * f(x_ref)
    keep_prev = (tile_in_seq != 0).astype(_F32)
    keep_next = (tile_in_seq != tiles_per_seq - 1).astype(_F32)
    halo = cp_ref.shape[0]
    u_prev = (f(cp_ref) * f(xp_ref))[halo - 1:halo, :] * keep_prev
    u_next = (f(cn_ref) * f(xn_ref))[0:1, :] * keep_next
    up, dn = _neighbour_rows(u, u_prev, u_next)
    w = w_ref[...]
    conv = up * w[0:1, :] + u * w[1:2, :] + dn * w[2:3, :] + bias_ref[...]
    ya = f(b_ref) * conv
    beta = beta_ref[...]
    groups = []
    for g in range(ya.shape[1] // CONV_GROUP):
        sl = slice(g * CONV_GROUP, (g + 1) * CONV_GROUP)
        yg = ya[:, sl]
        ms = jnp.mean(yg * yg, axis=-1, keepdims=True)
        groups.append(yg * lax.rsqrt(ms + EPS) * beta[:, sl])
    return jnp.concatenate(groups, axis=1)


def _out_proj_body(tiles_per_seq, n_col, b_ref, c_ref, xc_ref, cp_ref, xp_ref, cn_ref, xn_ref, cw_ref, cb_ref, beta_ref,
                   yr_hbm, wa_ref, wr_ref, x_ref, gain_ref, o_ref, xg_ref, ssq_ref, ya_ref, yr_buf_ref, yr_sem_ref):
    r = pl.program_id(0)
    j = pl.program_id(1)
    n_row = pl.num_programs(0) - 1
    cw = b_ref.shape[1]
    yr_ref = _prefetched_row_tile(n_col, yr_hbm, yr_buf_ref, yr_sem_ref, warm_up=True)

    def mix():
        ya = _conv_mixer_tile(r % tiles_per_seq, tiles_per_seq, b_ref, c_ref, xc_ref, cp_ref, xp_ref, cn_ref, xn_ref,
                              cw_ref, cb_ref, beta_ref)
        ya_ref[r % 2, :, pl.ds(pl.multiple_of(j * cw, cw), cw)] = ya.astype(ya_ref.dtype)

    def project():
        y = _dot(ya_ref[(r - 1) % 2], wa_ref[...]) + _dot(yr_ref[...], wr_ref[...])
        x_new = x_ref[...] + y
        o_ref[...] = x_new
        _emit_norm_inputs(x_new, gain_ref, xg_ref, ssq_ref, j)

    @pl.when(r == 0)
    def _():
        mix()

    @pl.when((r > 0) & (r < n_row))
    def _():
        mix()
        project()

    @pl.when(r == n_row)
    def _():
        project()


def _out_proj(proj, yr, w_tiles, conv_w, conv_b, beta, layer, x, gain, seq, tm=1024):
    m = proj.shape[0]
    depth, _, ka = conv_w.shape
    kr = yr.shape[1]
    n_col, _, tn = w_tiles.shape
    n = n_col * tn
    assert ka == kr
    tm = _fit(seq, tm)
    n_row = m // tm
    cw = ka // n_col
    assert cw * n_col == ka and cw % CONV_GROUP == 0
    assert (tm // n_col) * n_col == tm and (tm // n_col) % BF16_SUBLANES == 0
    nb = ka // cw
    halo = BF16_SUBLANES
    per = tm // halo
    mix_tile = lambda r: jnp.minimum(r, n_row - 1)
    out_tile = lambda r: jnp.maximum(r - 1, 0)
    main = lambda off: pl.BlockSpec((tm, cw), lambda r, j: (mix_tile(r), off * nb + j))
    before = lambda off: pl.BlockSpec((halo, cw), lambda r, j: (jnp.maximum(mix_tile(r) * per - 1, 0), off * nb + j))
    after = lambda off: pl.BlockSpec(
        (halo, cw), lambda r, j: (jnp.minimum((mix_tile(r) + 1) * per, m // halo - 1), off * nb + j))
    vec = lambda rows: pl.BlockSpec((None, rows, cw), lambda r, j: (layer, 0, j))
    tile = pl.BlockSpec((tm, tn), lambda r, j: (out_tile(r), jnp.where(r == 0, 0, j)))
    return pl.pallas_call(
        functools.partial(_out_proj_body, seq // tm, n_col),
        grid=(n_row + 1, n_col),
        in_specs=[main(0), main(1), main(2), before(1), before(2), after(1), after(2), vec(3), vec(1), vec(1),
                  pl.BlockSpec(memory_space=pl.ANY),
                  pl.BlockSpec((None, ka, tn), lambda r, j: (j, 0, 0)),
                  pl.BlockSpec((None, kr, tn), lambda r, j: (j, 1, 0)),
                  tile,
                  pl.BlockSpec((1, tn), lambda r, j: (0, j))],
        out_specs=[tile, tile, pl.BlockSpec((tm, SSQ_LANES), lambda r, j: (out_tile(r), 0))],
        out_shape=[jax.ShapeDtypeStruct((m, n), _F32), jax.ShapeDtypeStruct((m, n), _BF16),
                   jax.ShapeDtypeStruct((m, SSQ_LANES), _F32)],
        scratch_shapes=[pltpu.VMEM((2, tm, ka), _BF16), pltpu.VMEM((2, tm, kr), yr.dtype),
                        pltpu.SemaphoreType.DMA((2,))],
        compiler_params=_params("arbitrary", "arbitrary"),
        name="out_proj",
    )(proj, proj, proj, proj, proj, proj, proj, conv_w, conv_b.reshape(depth, 1, ka), beta.reshape(depth, 1, ka),
      yr, w_tiles, w_tiles, x, gain.reshape(1, n))


def _rope_tables_body(pos_ref, inv_ref, cos_ref, sin_ref):
    ang = pos_ref[...].astype(_F32) * inv_ref[...]
    cos_ref[...] = jnp.cos(ang)
    sin_ref[...] = jnp.sin(ang)


def _rope_tables(positions, tm=1024):
    m = positions.size
    tm = _fit(m, tm)
    half = HEAD_DIM // 2
    inv_freq = ROPE_BASE ** (-jnp.arange(half, dtype=_F32) / half)
    out = jax.ShapeDtypeStruct((m, half), _F32)
    return pl.pallas_call(
        _rope_tables_body,
        grid=(m // tm,),
        in_specs=[pl.BlockSpec((tm, 1), lambda i: (i, 0)),
                  pl.BlockSpec((1, half), lambda i: (0, 0))],
        out_specs=[pl.BlockSpec((tm, half), lambda i: (i, 0))] * 2,
        out_shape=[out, out],
        compiler_params=_params("parallel"),
        name="rope_tables",
    )(positions.reshape(m, 1), inv_freq.reshape(1, half))


def _rotary(t, cos, sin):
    half = t.shape[-1] // 2
    t1, t2 = t[:, :half], t[:, half:]
    return jnp.concatenate([t1 * cos - t2 * sin, t2 * cos + t1 * sin], axis=-1)


def _retention_body(n_heads, logit_ref, q_ref, k_ref, v_ref, g_ref, cos_ref, sin_ref, gn_ref, wo_ref,
                    o_ref, wo_tiles_ref, krot_ref, kvf_ref, kvb_ref, sf_ref, sb_ref, state_ref):
    _store_weight_tiles(wo_ref, wo_tiles_ref)

    h = pl.program_id(1)
    seq, d = q_ref.shape
    c = CHUNK
    n_chunks = seq // c

    def log_gamma(shape, direction):
        return jax.nn.log_sigmoid(jnp.full(shape, logit_ref[direction * n_heads + h], _F32))

    def chunk_rows(n):
        return pl.ds(pl.multiple_of(n * c, c), c)

    pos_cd = lax.broadcasted_iota(jnp.int32, (c, d), 0).astype(_F32)
    lf = log_gamma((c, d), 0)
    lb = log_gamma((c, d), 1)

    wk_f = jnp.exp(lf * (c - 1.0 - pos_cd))
    wk_b = jnp.exp(lb * pos_cd)

    def summarise(n, carry):
        rows = chunk_rows(n)
        k = _rotary(k_ref[rows, :].astype(_F32), cos_ref[rows, :], sin_ref[rows, :])
        krot_ref[rows, :] = k.astype(_BF16)
        v = v_ref[rows, :]
        contract_rows = (((0,), (0,)), ((), ()))
        kvf_ref[n] = lax.dot_general((k * wk_f).astype(_BF16), v, contract_rows, preferred_element_type=_F32)
        kvb_ref[n] = lax.dot_general((k * wk_b).astype(_BF16), v, contract_rows, preferred_element_type=_F32)
        return carry

    lax.fori_loop(0, n_chunks, summarise, 0, unroll=4)

    def scan(summary_ref, out_ref, log_g, order):
        g_chunk = jnp.exp(log_g * float(c))
        state_ref[...] = jnp.zeros_like(state_ref)

        def step(t, carry):
            n = order(t)
            state = state_ref[...]
            out_ref[n] = state.astype(_BF16)
            state_ref[...] = state * g_chunk + summary_ref[n]
            return carry

        lax.fori_loop(0, n_chunks, step, 0, unroll=4)

    scan(kvf_ref, sf_ref, log_gamma((1, d), 0), lambda t: t)
    scan(kvb_ref, sb_ref, log_gamma((1, d), 1), lambda t: n_chunks - 1 - t)

    row = lax.broadcasted_iota(jnp.int32, (c, c), 0)
    col = lax.broadcasted_iota(jnp.int32, (c, c), 1)
    rel = (row - col).astype(_F32)
    scale = HEAD_DIM ** -0.5
    dec = jnp.where(rel >= 0,
                    jnp.exp(log_gamma((c, c), 0) * jnp.maximum(rel, 0.0)),
                    jnp.exp(log_gamma((c, c), 1) * jnp.maximum(-rel, 0.0))) * scale
    wq_f = jnp.exp(lf * (pos_cd + 1.0)) * scale
    wq_b = jnp.exp(lb * (c - pos_cd)) * scale
    gain = gn_ref[...]

    def output(n, carry):
        rows = chunk_rows(n)
        q = _rotary(q_ref[rows, :].astype(_F32), cos_ref[rows, :], sin_ref[rows, :]).astype(_BF16)
        scores = lax.dot_general(q, krot_ref[rows, :], (((1,), (1,)), ((), ())),
                                 preferred_element_type=_F32) * dec
        o = _dot(scores.astype(_BF16), v_ref[rows, :])
        o = o + wq_f * _dot(q, sf_ref[n]) + wq_b * _dot(q, sb_ref[n])
        mu = jnp.mean(o, axis=-1, keepdims=True)
        cen = o - mu
        var = jnp.mean(cen * cen, axis=-1, keepdims=True)
        normed = cen * lax.rsqrt(var + EPS) * gain
        gate = g_ref[rows, :].astype(_F32)
        o_ref[rows, :] = (normed * jax.nn.silu(gate)).astype(o_ref.dtype)
        return carry

    lax.fori_loop(0, n_chunks, output, 0, unroll=4)


def _retention(proj, cos, sin, decay_logit, gn_g, w_out, layer, batch, seq, col0):
    m = proj.shape[0]
    depth, rw = gn_g.shape
    n_heads = rw // HEAD_DIM
    _, wo_rows, wo_cols = w_out.shape
    tn = _fit(wo_cols, PROJ_TILE)
    slab = wo_rows // (batch * n_heads)
    assert slab * batch * n_heads == wo_rows and slab % BF16_SUBLANES == 0
    first = col0 // HEAD_DIM
    n_chunks = seq // CHUNK
    part = lambda p: pl.BlockSpec((seq, HEAD_DIM), lambda b, h: (b, first + p * n_heads + h))
    table = pl.BlockSpec((seq, HEAD_DIM // 2), lambda b, h: (b, 0), pipeline_mode=pl.Buffered(1))
    summaries = pltpu.VMEM((n_chunks, HEAD_DIM, HEAD_DIM), _F32)
    states = pltpu.VMEM((n_chunks, HEAD_DIM, HEAD_DIM), _BF16)
    return pl.pallas_call(
        functools.partial(_retention_body, n_heads),
        grid=(batch, n_heads),
        in_specs=[pl.BlockSpec(memory_space=pltpu.SMEM),
                  part(0), part(1), part(2), part(3), table, table,
                  pl.BlockSpec((None, 1, HEAD_DIM), lambda b, h: (layer, 0, h)),
                  pl.BlockSpec((None, slab, wo_cols), lambda b, h: (layer, b * n_heads + h, 0))],
        out_specs=[pl.BlockSpec((seq, HEAD_DIM), lambda b, h: (b, h)),
                   pl.BlockSpec((wo_cols // tn, slab, tn), lambda b, h: (0, b * n_heads + h, 0))],
        out_shape=[jax.ShapeDtypeStruct((m, rw), _BF16),
                   jax.ShapeDtypeStruct((wo_cols // tn, wo_rows, tn), _BF16)],
        scratch_shapes=[pltpu.VMEM((seq, HEAD_DIM), _BF16), summaries, summaries, states, states,
                        pltpu.VMEM((HEAD_DIM, HEAD_DIM), _F32)],
        compiler_params=_params("parallel", "parallel"),
        name="retention",
    )(decay_logit[layer].reshape(-1), proj, proj, proj, proj, cos, sin, gn_g.reshape(depth, 1, rw), w_out)


def _ffn_up_body(tiles_per_seq, n_sub, h_ref, hp_ref, hn_ref, ssq_ref, ssqp_ref, ssqn_ref,
                 wg_ref, wu_ref, cw_ref, cb_ref, wd_ref, o_ref, wd_tiles_ref):
    @pl.when(pl.program_id(0) == 0)
    def _():
        _store_weight_tiles(wd_ref, wd_tiles_ref)

    tm, d_model = h_ref.shape
    halo = hp_ref.shape[0]
    wg = wg_ref[...].astype(_BF16)
    wu = wu_ref[...].astype(_BF16)
    w = cw_ref[...]
    bias = cb_ref[...]
    t = pl.program_id(0) % tiles_per_seq
    scale_prev = _row_scale(ssqp_ref[...], d_model)[-1:, :] * (t != 0).astype(_F32)
    scale_next = _row_scale(ssqn_ref[...], d_model)[0:1, :] * (t != tiles_per_seq - 1).astype(_F32)

    sub = tm // n_sub
    rows = lambda s: slice(s * sub, (s + 1) * sub)
    scale = functools.cache(lambda s: _row_scale(ssq_ref[rows(s), :], d_model))

    def gate_pre(s):
        parts = ([hp_ref[...]] if s == 0 else []) + [h_ref[rows(s), :]] + ([hn_ref[...]] if s == n_sub - 1 else [])
        g = _dot(parts[0] if len(parts) == 1 else jnp.concatenate(parts, axis=0), wg)
        lo = halo if s == 0 else 0
        before = g[lo - 1:lo, :] * scale_prev if s == 0 else None
        after = g[lo + sub:lo + sub + 1, :] * scale_next if s == n_sub - 1 else None
        return g[lo:lo + sub, :] * scale(s), before, after

    order = list(range(1, n_sub - 1)) + [0, n_sub - 1]
    gs, edge = {}, {}
    for s in order:
        gs[s], before, after = gate_pre(s)
        if before is not None:
            edge["prev"] = before
        if after is not None:
            edge["next"] = after
    for s in range(n_sub):
        u = _dot(h_ref[rows(s), :], wu) * scale(s)
        prev_row = edge["prev"] if s == 0 else gs[s - 1][sub - 1:sub, :]
        next_row = edge["next"] if s == n_sub - 1 else gs[s + 1][0:1, :]
        g_up, g_dn = _neighbour_rows(gs[s], prev_row, next_row)
        gate = g_up * w[0:1, :] + gs[s] * w[1:2, :] + g_dn * w[2:3, :] + bias
        o_ref[rows(s), :] = (jax.nn.silu(gate) * u).astype(o_ref.dtype)


def _ffn_up(h, ssq, w_gate, w_up, conv_w, conv_b, w_down, layer, seq, tm=2048, tn=256, sub=512):
    m, k = h.shape
    depth, _, ff = w_gate.shape
    d_out = w_down.shape[2]
    tm, tn = _fit(seq, tm), _fit(ff, tn)
    n_col = ff // tn
    dt = _fit(d_out, PROJ_TILE)
    halo = BF16_SUBLANES
    f32_rows = 8
    once = dict(pipeline_mode=pl.Buffered(1))
    before = lambda rows: (lambda i, j: (jnp.maximum(i * (tm // rows) - 1, 0), 0))
    after = lambda rows: (lambda i, j: (jnp.minimum((i + 1) * (tm // rows), m // rows - 1), 0))
    weight = pl.BlockSpec((None, k, tn), lambda i, j: (layer, 0, j))
    vec = lambda rows: pl.BlockSpec((None, rows, tn), lambda i, j: (layer, 0, j))
    return pl.pallas_call(
        functools.partial(_ffn_up_body, seq // tm, tm // _fit(tm, sub)),
        grid=(m // tm, ff // tn),
        in_specs=[pl.BlockSpec((tm, k), lambda i, j: (i, 0), **once),
                  pl.BlockSpec((halo, k), before(halo)),
                  pl.BlockSpec((halo, k), after(halo)),
                  pl.BlockSpec((tm, SSQ_LANES), lambda i, j: (i, 0), **once),
                  pl.BlockSpec((f32_rows, SSQ_LANES), before(f32_rows)),
                  pl.BlockSpec((f32_rows, SSQ_LANES), after(f32_rows)),
                  weight, weight, vec(3), vec(1),
                  pl.BlockSpec((None, tn, d_out), lambda i, j: (layer, jnp.where(i == 0, j, 0), 0))],
        out_specs=[pl.BlockSpec((tm, tn), lambda i, j: (i, j)),
                   pl.BlockSpec((d_out // dt, tn, dt), lambda i, j: (0, jnp.where(i == 0, j, n_col - 1), 0))],
        out_shape=[jax.ShapeDtypeStruct((m, ff), _BF16), jax.ShapeDtypeStruct((d_out // dt, ff, dt), _BF16)],
        compiler_params=_params("arbitrary", "arbitrary"),
        name="ffn_up",
    )(h, h, h, ssq, ssq, ssq, w_gate, w_up, conv_w, conv_b.reshape(depth, 1, ff), w_down)


def kernel(x, positions, norm1_g, w_in, conv_a_w, conv_a_b, beta_a, ret_decay_logit, ret_gn_g, w_out,
           norm2_g, w_gate, w_up, ffn_conv_w, ffn_conv_b, w_down, norm_f_g):
    batch, seq, d_model = x.shape
    depth = w_in.shape[0]
    conv_width = conv_a_w.shape[-1]
    m = batch * seq

    xf = x.reshape(m, d_model)
    cos, sin = _rope_tables(positions)
    xg, ssq = _stream_in(xf, norm1_g[0])

    for l in range(depth):
        proj = _in_proj(xg, ssq, w_in, l)
        yr, w_o = _retention(proj, cos, sin, ret_decay_logit, ret_gn_g, w_out, l, batch, seq, 3 * conv_width)
        xf, xg, ssq = _out_proj(proj, yr, w_o, conv_a_w, conv_a_b, beta_a, l, xf, norm2_g[l], seq)
        hidden, w_dn = _ffn_up(xg, ssq, w_gate, w_up, ffn_conv_w, ffn_conv_b, w_down, l, seq)
        if l + 1 < depth:
            xf, xg, ssq = _down_proj(hidden, w_dn, xf, norm1_g[l + 1])
        else:
            xf = _down_proj(hidden, w_dn, xf)

    return _rmsnorm(xf, norm_f_g, x.dtype).reshape(batch, seq, d_model)
```

```python
import functools

import jax
import jax.numpy as jnp
from jax import lax
from jax.experimental import pallas as pl
from jax.experimental.pallas import tpu as pltpu

EPS = 1e-6
ROPE_BASE = 10000.0
HEAD_DIM = 256
CONV_GROUP = 128
CHUNK = 256
BF16_SUBLANES = 16
V7X_VMEM_LIMIT = 56 * 1024 * 1024

_BF16 = jnp.bfloat16
_F32 = jnp.float32


def _params(*sem):
    return pltpu.CompilerParams(dimension_semantics=sem, vmem_limit_bytes=V7X_VMEM_LIMIT)


def _dot(a, b):
    return jnp.dot(a, b, preferred_element_type=_F32)


def _fit(dim, tile):
    return tile if dim % tile == 0 else dim


SSQ_LANES = 128


def _row_scale(ssq, width):
    return lax.rsqrt(jnp.sum(ssq, axis=-1, keepdims=True) / width + EPS)


def _emit_norm_inputs(x_new, gain_ref, xg_ref, ssq_ref, col_step):
    xg_ref[...] = (x_new * gain_ref[...]).astype(xg_ref.dtype)
    sq = x_new * x_new
    part = sq[:, 0:SSQ_LANES]
    for c in range(1, sq.shape[1] // SSQ_LANES):
        part = part + sq[:, c * SSQ_LANES:(c + 1) * SSQ_LANES]

    @pl.when(col_step == 0)
    def _():
        ssq_ref[...] = part

    @pl.when(col_step != 0)
    def _():
        ssq_ref[...] += part


def _norm_inputs_out(m, n, tm, tn):
    specs = [pl.BlockSpec((tm, tn), lambda i, j: (i, j)), pl.BlockSpec((tm, SSQ_LANES), lambda i, j: (i, 0))]
    shapes = [jax.ShapeDtypeStruct((m, n), _BF16), jax.ShapeDtypeStruct((m, SSQ_LANES), _F32)]
    return specs, shapes


def _stream_in_body(x_ref, g_ref, xg_ref, ssq_ref):
    _emit_norm_inputs(x_ref[...], g_ref, xg_ref, ssq_ref, pl.program_id(1))


def _stream_in(x, gain, tm=256):
    m, d = x.shape
    tm = _fit(m, tm)
    specs, shapes = _norm_inputs_out(m, d, tm, d)
    return pl.pallas_call(
        _stream_in_body,
        grid=(m // tm, 1),
        in_specs=[pl.BlockSpec((tm, d), lambda i, j: (i, 0)),
                  pl.BlockSpec((1, d), lambda i, j: (0, 0))],
        out_specs=specs,
        out_shape=shapes,
        compiler_params=_params("parallel", "arbitrary"),
        name="stream_in",
    )(x, gain.reshape(1, d))


def _rmsnorm_body(x_ref, g_ref, o_ref):
    x = x_ref[...]
    ms = jnp.mean(x * x, axis=-1, keepdims=True)
    o_ref[...] = (x * lax.rsqrt(ms + EPS) * g_ref[...]).astype(o_ref.dtype)


def _rmsnorm(x, g, out_dtype, tm=256):
    m, d = x.shape
    tm = _fit(m, tm)
    return pl.pallas_call(
        _rmsnorm_body,
        grid=(m // tm,),
        in_specs=[pl.BlockSpec((tm, d), lambda i: (i, 0)),
                  pl.BlockSpec((1, d), lambda i: (0, 0))],
        out_specs=pl.BlockSpec((tm, d), lambda i: (i, 0)),
        out_shape=jax.ShapeDtypeStruct((m, d), out_dtype),
        compiler_params=_params("parallel"),
        name="rmsnorm",
    )(x, g.reshape(1, d))


def _in_proj_body(a_ref, ssq_ref, wslice_ref, o_ref, w_bf16_ref):
    j = pl.program_id(0)
    i = pl.program_id(1)
    rows = wslice_ref.shape[0]

    def round_slice():
        dst = pl.ds(pl.multiple_of(i * rows, rows), rows)
        w_bf16_ref[j % 2, dst, :] = wslice_ref[...].astype(w_bf16_ref.dtype)

    @pl.when(j == 0)
    def _():
        round_slice()

    @pl.when(j > 0)
    def _():
        scale = _row_scale(ssq_ref[...], a_ref.shape[1])
        o_ref[...] = (_dot(a_ref[...], w_bf16_ref[(j - 1) % 2]) * scale).astype(o_ref.dtype)
        round_slice()


def _in_proj(a, ssq, w, layer, tm=1024, tn=1024):
    m, k = a.shape
    n = w.shape[2]
    tm, tn = _fit(m, tm), _fit(n, tn)
    n_row, n_col = m // tm, n // tn
    rows = k // n_row
    assert rows * n_row == k and rows % BF16_SUBLANES == 0
    row_tile = lambda j, i: jnp.where(j > 0, i, 0)
    return pl.pallas_call(
        _in_proj_body,
        grid=(n_col + 1, n_row),
        in_specs=[pl.BlockSpec((tm, k), lambda j, i: (row_tile(j, i), 0)),
                  pl.BlockSpec((tm, SSQ_LANES), lambda j, i: (row_tile(j, i), 0)),
                  pl.BlockSpec((None, rows, tn), lambda j, i: (layer, i, jnp.minimum(j, n_col - 1)))],
        out_specs=pl.BlockSpec((tm, tn), lambda j, i: (row_tile(j, i), jnp.maximum(j - 1, 0))),
        out_shape=jax.ShapeDtypeStruct((m, n), _BF16),
        scratch_shapes=[pltpu.VMEM((2, k, tn), _BF16)],
        compiler_params=_params("arbitrary", "arbitrary"),
        name="in_proj",
    )(a, ssq, w)


def _prefetched_row_tile(n_col, a_hbm, buf_ref, sem_ref):
    i = pl.program_id(0)
    j = pl.program_id(1)
    tm = buf_ref.shape[1]
    chunk = tm // n_col

    def chunk_copy(tile, c):
        src = a_hbm.at[pl.ds(tile * tm + c * chunk, chunk), :]
        dst = buf_ref.at[tile % 2, pl.ds(c * chunk, chunk), :]
        return pltpu.make_async_copy(src, dst, sem_ref.at[tile % 2])

    @pl.when((i == 0) & (j == 0))
    def _():
        for c in range(n_col):
            chunk_copy(0, c).start()

    @pl.when(i + 1 < pl.num_programs(0))
    def _():
        chunk_copy(i + 1, j).start()

    @pl.when(j == 0)
    def _():
        for c in range(n_col):
            chunk_copy(i, c).wait()

    return buf_ref.at[i % 2]


def _down_proj_body(n_col, a_hbm, w_ref, x_ref, gain_ref, o_ref, xg_ref, ssq_ref, buf_ref, sem_ref):
    a_ref = _prefetched_row_tile(n_col, a_hbm, buf_ref, sem_ref)
    x_new = x_ref[...] + _dot(a_ref[...], w_ref[...])
    o_ref[...] = x_new
    _emit_norm_inputs(x_new, gain_ref, xg_ref, ssq_ref, pl.program_id(1))


def _down_proj_last_body(n_col, a_hbm, w_ref, x_ref, o_ref, buf_ref, sem_ref):
    a_ref = _prefetched_row_tile(n_col, a_hbm, buf_ref, sem_ref)
    o_ref[...] = x_ref[...] + _dot(a_ref[...], w_ref[...])


def _down_proj(a, w_tiles, x, gain=None, tm=512):
    m, k = a.shape
    n_col, _, tn = w_tiles.shape
    n = n_col * tn
    tm = _fit(m, tm)
    assert (tm // n_col) * n_col == tm and (tm // n_col) % BF16_SUBLANES == 0
    tile = pl.BlockSpec((tm, tn), lambda i, j: (i, j))
    in_specs = [pl.BlockSpec(memory_space=pl.ANY), pl.BlockSpec((None, k, tn), lambda i, j: (j, 0, 0)), tile]
    scratch = [pltpu.VMEM((2, tm, k), a.dtype), pltpu.SemaphoreType.DMA((2,))]
    params = _params("arbitrary", "arbitrary")
    if gain is None:
        return pl.pallas_call(
            functools.partial(_down_proj_last_body, n_col),
            grid=(m // tm, n_col),
            in_specs=in_specs,
            out_specs=tile,
            out_shape=jax.ShapeDtypeStruct((m, n), _F32),
            scratch_shapes=scratch,
            compiler_params=params,
            name="down_proj_last",
        )(a, w_tiles, x)
    specs, shapes = _norm_inputs_out(m, n, tm, tn)
    return pl.pallas_call(
        functools.partial(_down_proj_body, n_col),
        grid=(m // tm, n_col),
        in_specs=in_specs + [pl.BlockSpec((1, tn), lambda i, j: (0, j))],
        out_specs=[tile] + specs,
        out_shape=[jax.ShapeDtypeStruct((m, n), _F32)] + shapes,
        scratch_shapes=scratch,
        compiler_params=params,
        name="down_proj",
    )(a, w_tiles, x, gain.reshape(1, n))


PROJ_TILE = 512


def _store_weight_tiles(w_ref, tiles_ref):
    tn = tiles_ref.shape[2]
    for c in range(tiles_ref.shape[0]):
        tiles_ref[c] = w_ref[:, c * tn:(c + 1) * tn].astype(tiles_ref.dtype)


def _neighbour_rows(x, prev_row, next_row):
    tm = x.shape[0]
    rows = lax.broadcasted_iota(jnp.int32, x.shape, 0)
    up = jnp.where(rows == 0, prev_row, pltpu.roll(x, 1, 0))
    dn = jnp.where(rows == tm - 1, next_row, pltpu.roll(x, tm - 1, 0))
    return up, dn


def _conv_mixer_tile(tile_in_seq, tiles_per_seq, b_ref, c_ref, x_ref, cp_ref, xp_ref, cn_ref, xn_ref,
                     w_ref, bias_ref, beta_ref):
    f = lambda r: r[...].astype(_F32)
    u = f(c_ref) * f(x_ref)
    keep_prev = (tile_in_seq != 0).astype(_F32)
    keep_next = (tile_in_seq != tiles_per_seq - 1).astype(_F32)
    halo = cp_ref.shape[0]
    u_prev = (f(cp_ref) * f(xp_ref))[halo - 1:halo, :] * keep_prev
    u_next = (f(cn_ref) * f(xn_ref))[0:1, :] * keep_next
    up, dn = _neighbour_rows(u, u_prev, u_next)
    w = w_ref[...]
    conv = up * w[0:1, :] + u * w[1:2, :] + dn * w[2:3, :] + bias_ref[...]
    ya = f(b_ref) * conv
    beta = beta_ref[...]
    groups = []
    for g in range(ya.shape[1] // CONV_GROUP):
        sl = slice(g * CONV_GROUP, (g + 1) * CONV_GROUP)
        yg = ya[:, sl]
        ms = jnp.mean(yg * yg, axis=-1, keepdims=True)
        groups.append(yg * lax.rsqrt(ms + EPS) * beta[:, sl])
    return jnp.concatenate(groups, axis=1)


def _out_proj_body(tiles_per_seq, b_ref, c_ref, xc_ref, cp_ref, xp_ref, cn_ref, xn_ref, cw_ref, cb_ref, beta_ref,
                   yr_ref, wa_ref, wr_ref, x_ref, gain_ref, o_ref, xg_ref, ssq_ref, ya_ref):
    r = pl.program_id(0)
    j = pl.program_id(1)
    n_row = pl.num_programs(0) - 1
    cw = b_ref.shape[1]

    def mix():
        ya = _conv_mixer_tile(r % tiles_per_seq, tiles_per_seq, b_ref, c_ref, xc_ref, cp_ref, xp_ref, cn_ref, xn_ref,
                              cw_ref, cb_ref, beta_ref)
        ya_ref[r % 2, :, pl.ds(pl.multiple_of(j * cw, cw), cw)] = ya.astype(ya_ref.dtype)

    def project():
        y = _dot(ya_ref[(r - 1) % 2], wa_ref[...]) + _dot(yr_ref[...], wr_ref[...])
        x_new = x_ref[...] + y
        o_ref[...] = x_new
        _emit_norm_inputs(x_new, gain_ref, xg_ref, ssq_ref, j)

    @pl.when(r == 0)
    def _():
        mix()

    @pl.when((r > 0) & (r < n_row))
    def _():
        mix()
        project()

    @pl.when(r == n_row)
    def _():
        project()


def _out_proj(proj, yr, w_tiles, conv_w, conv_b, beta, layer, x, gain, seq, tm=1024):
    m = proj.shape[0]
    depth, _, ka = conv_w.shape
    kr = yr.shape[1]
    n_col, _, tn = w_tiles.shape
    n = n_col * tn
    assert ka == kr
    tm = _fit(seq, tm)
    n_row = m // tm
    cw = ka // n_col
    assert cw * n_col == ka and cw % CONV_GROUP == 0
    nb = ka // cw
    halo = BF16_SUBLANES
    per = tm // halo
    mix_tile = lambda r: jnp.minimum(r, n_row - 1)
    out_tile = lambda r: jnp.maximum(r - 1, 0)
    main = lambda off: pl.BlockSpec((tm, cw), lambda r, j: (mix_tile(r), off * nb + j))
    before = lambda off: pl.BlockSpec((halo, cw), lambda r, j: (jnp.maximum(mix_tile(r) * per - 1, 0), off * nb + j))
    after = lambda off: pl.BlockSpec(
        (halo, cw), lambda r, j: (jnp.minimum((mix_tile(r) + 1) * per, m // halo - 1), off * nb + j))
    vec = lambda rows: pl.BlockSpec((None, rows, cw), lambda r, j: (layer, 0, j))
    tile = pl.BlockSpec((tm, tn), lambda r, j: (out_tile(r), jnp.where(r == 0, 0, j)))
    return pl.pallas_call(
        functools.partial(_out_proj_body, seq // tm),
        grid=(n_row + 1, n_col),
        in_specs=[main(0), main(1), main(2), before(1), before(2), after(1), after(2), vec(3), vec(1), vec(1),
                  pl.BlockSpec((tm, kr), lambda r, j: (out_tile(r), 0)),
                  pl.BlockSpec((None, ka, tn), lambda r, j: (j, 0, 0)),
                  pl.BlockSpec((None, kr, tn), lambda r, j: (j, 1, 0)),
                  tile,
                  pl.BlockSpec((1, tn), lambda r, j: (0, j))],
        out_specs=[tile, tile, pl.BlockSpec((tm, SSQ_LANES), lambda r, j: (out_tile(r), 0))],
        out_shape=[jax.ShapeDtypeStruct((m, n), _F32), jax.ShapeDtypeStruct((m, n), _BF16),
                   jax.ShapeDtypeStruct((m, SSQ_LANES), _F32)],
        scratch_shapes=[pltpu.VMEM((2, tm, ka), _BF16)],
        compiler_params=_params("arbitrary", "arbitrary"),
        name="out_proj",
    )(proj, proj, proj, proj, proj, proj, proj, conv_w, conv_b.reshape(depth, 1, ka), beta.reshape(depth, 1, ka),
      yr, w_tiles, w_tiles, x, gain.reshape(1, n))


def _rope_tables_body(pos_ref, inv_ref, cos_ref, sin_ref):
    ang = pos_ref[...].astype(_F32) * inv_ref[...]
    cos_ref[...] = jnp.cos(ang)
    sin_ref[...] = jnp.sin(ang)


def _rope_tables(positions, tm=1024):
    m = positions.size
    tm = _fit(m, tm)
    half = HEAD_DIM // 2
    inv_freq = ROPE_BASE ** (-jnp.arange(half, dtype=_F32) / half)
    out = jax.ShapeDtypeStruct((m, half), _F32)
    return pl.pallas_call(
        _rope_tables_body,
        grid=(m // tm,),
        in_specs=[pl.BlockSpec((tm, 1), lambda i: (i, 0)),
                  pl.BlockSpec((1, half), lambda i: (0, 0))],
        out_specs=[pl.BlockSpec((tm, half), lambda i: (i, 0))] * 2,
        out_shape=[out, out],
        compiler_params=_params("parallel"),
        name="rope_tables",
    )(positions.reshape(m, 1), inv_freq.reshape(1, half))


def _rotary(t, cos, sin):
    half = t.shape[-1] // 2
    t1, t2 = t[:, :half], t[:, half:]
    return jnp.concatenate([t1 * cos - t2 * sin, t2 * cos + t1 * sin], axis=-1)


def _retention_body(n_heads, logit_ref, q_ref, k_ref, v_ref, g_ref, cos_ref, sin_ref, gn_ref, wo_ref,
                    o_ref, wo_tiles_ref, krot_ref, kvf_ref, kvb_ref, sf_ref, sb_ref, state_ref):
    _store_weight_tiles(wo_ref, wo_tiles_ref)

    h = pl.program_id(1)
    seq, d = q_ref.shape
    c = CHUNK
    n_chunks = seq // c

    def log_gamma(shape, direction):
        return jax.nn.log_sigmoid(jnp.full(shape, logit_ref[direction * n_heads + h], _F32))

    def chunk_rows(n):
        return pl.ds(pl.multiple_of(n * c, c), c)

    pos_cd = lax.broadcasted_iota(jnp.int32, (c, d), 0).astype(_F32)
    lf = log_gamma((c, d), 0)
    lb = log_gamma((c, d), 1)

    wk_f = jnp.exp(lf * (c - 1.0 - pos_cd))
    wk_b = jnp.exp(lb * pos_cd)

    def summarise(n, carry):
        rows = chunk_rows(n)
        k = _rotary(k_ref[rows, :].astype(_F32), cos_ref[rows, :], sin_ref[rows, :])
        krot_ref[rows, :] = k.astype(_BF16)
        v = v_ref[rows, :]
        contract_rows = (((0,), (0,)), ((), ()))
        kvf_ref[n] = lax.dot_general((k * wk_f).astype(_BF16), v, contract_rows, preferred_element_type=_F32)
        kvb_ref[n] = lax.dot_general((k * wk_b).astype(_BF16), v, contract_rows, preferred_element_type=_F32)
        return carry

    lax.fori_loop(0, n_chunks, summarise, 0, unroll=4)

    def scan(summary_ref, out_ref, log_g, order):
        g_chunk = jnp.exp(log_g * float(c))
        state_ref[...] = jnp.zeros_like(state_ref)

        def step(t, carry):
            n = order(t)
            state = state_ref[...]
            out_ref[n] = state.astype(_BF16)
            state_ref[...] = state * g_chunk + summary_ref[n]
            return carry

        lax.fori_loop(0, n_chunks, step, 0, unroll=4)

    scan(kvf_ref, sf_ref, log_gamma((1, d), 0), lambda t: t)
    scan(kvb_ref, sb_ref, log_gamma((1, d), 1), lambda t: n_chunks - 1 - t)

    row = lax.broadcasted_iota(jnp.int32, (c, c), 0)
    col = lax.broadcasted_iota(jnp.int32, (c, c), 1)
    rel = (row - col).astype(_F32)
    scale = HEAD_DIM ** -0.5
    dec = jnp.where(rel >= 0,
                    jnp.exp(log_gamma((c, c), 0) * jnp.maximum(rel, 0.0)),
                    jnp.exp(log_gamma((c, c), 1) * jnp.maximum(-rel, 0.0))) * scale
    wq_f = jnp.exp(lf * (pos_cd + 1.0)) * scale
    wq_b = jnp.exp(lb * (c - pos_cd)) * scale
    gain = gn_ref[...]

    def output(n, carry):
        rows = chunk_rows(n)
        q = _rotary(q_ref[rows, :].astype(_F32), cos_ref[rows, :], sin_ref[rows, :]).astype(_BF16)
        scores = lax.dot_general(q, krot_ref[rows, :], (((1,), (1,)), ((), ())),
                                 preferred_element_type=_F32) * dec
        o = _dot(scores.astype(_BF16), v_ref[rows, :])
        o = o + wq_f * _dot(q, sf_ref[n]) + wq_b * _dot(q, sb_ref[n])
        mu = jnp.mean(o, axis=-1, keepdims=True)
        cen = o - mu
        var = jnp.mean(cen * cen, axis=-1, keepdims=True)
        normed = cen * lax.rsqrt(var + EPS) * gain
        gate = g_ref[rows, :].astype(_F32)
        o_ref[rows, :] = (normed * jax.nn.silu(gate)).astype(o_ref.dtype)
        return carry

    lax.fori_loop(0, n_chunks, output, 0, unroll=8)


def _retention(proj, cos, sin, decay_logit, gn_g, w_out, layer, batch, seq, col0):
    m = proj.shape[0]
    depth, rw = gn_g.shape
    n_heads = rw // HEAD_DIM
    _, wo_rows, wo_cols = w_out.shape
    tn = _fit(wo_cols, PROJ_TILE)
    slab = wo_rows // (batch * n_heads)
    assert slab * batch * n_heads == wo_rows and slab % BF16_SUBLANES == 0
    first = col0 // HEAD_DIM
    n_chunks = seq // CHUNK
    part = lambda p: pl.BlockSpec((seq, HEAD_DIM), lambda b, h: (b, first + p * n_heads + h))
    table = pl.BlockSpec((seq, HEAD_DIM // 2), lambda b, h: (b, 0), pipeline_mode=pl.Buffered(1))
    summaries = pltpu.VMEM((n_chunks, HEAD_DIM, HEAD_DIM), _F32)
    states = pltpu.VMEM((n_chunks, HEAD_DIM, HEAD_DIM), _BF16)
    return pl.pallas_call(
        functools.partial(_retention_body, n_heads),
        grid=(batch, n_heads),
        in_specs=[pl.BlockSpec(memory_space=pltpu.SMEM),
                  part(0), part(1), part(2), part(3), table, table,
                  pl.BlockSpec((None, 1, HEAD_DIM), lambda b, h: (layer, 0, h)),
                  pl.BlockSpec((None, slab, wo_cols), lambda b, h: (layer, b * n_heads + h, 0))],
        out_specs=[pl.BlockSpec((seq, HEAD_DIM), lambda b, h: (b, h)),
                   pl.BlockSpec((wo_cols // tn, slab, tn), lambda b, h: (0, b * n_heads + h, 0))],
        out_shape=[jax.ShapeDtypeStruct((m, rw), _BF16),
                   jax.ShapeDtypeStruct((wo_cols // tn, wo_rows, tn), _BF16)],
        scratch_shapes=[pltpu.VMEM((seq, HEAD_DIM), _BF16), summaries, summaries, states, states,
                        pltpu.VMEM((HEAD_DIM, HEAD_DIM), _F32)],
        compiler_params=_params("parallel", "parallel"),
        name="retention",
    )(decay_logit[layer].reshape(-1), proj, proj, proj, proj, cos, sin, gn_g.reshape(depth, 1, rw), w_out)


def _ffn_up_body(tiles_per_seq, n_sub, h_ref, hn_ref, ssq_ref, ssqn_ref,
                 wg_ref, wu_ref, cw_ref, cb_ref, wd_ref, o_ref, wd_tiles_ref, carry_ref):
    i = pl.program_id(0)
    j = pl.program_id(1)

    @pl.when(i == 0)
    def _():
        _store_weight_tiles(wd_ref, wd_tiles_ref)
        carry_ref[j] = jnp.zeros(carry_ref.shape[1:], carry_ref.dtype)

    tm, d_model = h_ref.shape
    halo = hn_ref.shape[0]
    wg = wg_ref[...].astype(_BF16)
    wu = wu_ref[...].astype(_BF16)
    w = cw_ref[...]
    bias = cb_ref[...]
    t = i % tiles_per_seq
    keep = carry_ref.shape[1]
    prev_row = carry_ref[j][keep - 1:keep, :] * (t != 0).astype(_F32)
    scale_next = _row_scale(ssqn_ref[...], d_model)[0:1, :] * (t != tiles_per_seq - 1).astype(_F32)

    sub = tm // n_sub
    rows = lambda s: slice(s * sub, (s + 1) * sub)
    scale = functools.cache(lambda s: _row_scale(ssq_ref[rows(s), :], d_model))

    def gate_pre(s):
        if s < n_sub - 1:
            return _dot(h_ref[rows(s), :], wg) * scale(s), None
        g = _dot(jnp.concatenate([h_ref[rows(s), :], hn_ref[...]], axis=0), wg)
        return g[:sub, :] * scale(s), g[sub:sub + 1, :] * scale_next

    gs = {}
    for s in range(n_sub):
        gs[s], next_edge = gate_pre(s)
    carry_ref[j] = gs[n_sub - 1][sub - keep:, :]
    for s in range(n_sub):
        u = _dot(h_ref[rows(s), :], wu) * scale(s)
        above = prev_row if s == 0 else gs[s - 1][sub - 1:sub, :]
        below = next_edge if s == n_sub - 1 else gs[s + 1][0:1, :]
        g_up, g_dn = _neighbour_rows(gs[s], above, below)
        gate = g_up * w[0:1, :] + gs[s] * w[1:2, :] + g_dn * w[2:3, :] + bias
        o_ref[rows(s), :] = (jax.nn.silu(gate) * u).astype(o_ref.dtype)


def _ffn_up(h, ssq, w_gate, w_up, conv_w, conv_b, w_down, layer, seq, tm=2048, tn=256, sub=512):
    m, k = h.shape
    depth, _, ff = w_gate.shape
    d_out = w_down.shape[2]
    tm, tn = _fit(seq, tm), _fit(ff, tn)
    n_col = ff // tn
    dt = _fit(d_out, PROJ_TILE)
    halo = BF16_SUBLANES
    f32_rows = 8
    once = dict(pipeline_mode=pl.Buffered(1))
    after = lambda rows: (lambda i, j: (jnp.minimum((i + 1) * (tm // rows), m // rows - 1), 0))
    weight = pl.BlockSpec((None, k, tn), lambda i, j: (layer, 0, j))
    vec = lambda rows: pl.BlockSpec((None, rows, tn), lambda i, j: (layer, 0, j))
    return pl.pallas_call(
        functools.partial(_ffn_up_body, seq // tm, tm // _fit(tm, sub)),
        grid=(m // tm, ff // tn),
        in_specs=[pl.BlockSpec((tm, k), lambda i, j: (i, 0), **once),
                  pl.BlockSpec((halo, k), after(halo)),
                  pl.BlockSpec((tm, SSQ_LANES), lambda i, j: (i, 0), **once),
                  pl.BlockSpec((f32_rows, SSQ_LANES), after(f32_rows)),
                  weight, weight, vec(3), vec(1),
                  pl.BlockSpec((None, tn, d_out), lambda i, j: (layer, jnp.where(i == 0, j, 0), 0))],
        out_specs=[pl.BlockSpec((tm, tn), lambda i, j: (i, j)),
                   pl.BlockSpec((d_out // dt, tn, dt), lambda i, j: (0, jnp.where(i == 0, j, n_col - 1), 0))],
        out_shape=[jax.ShapeDtypeStruct((m, ff), _BF16), jax.ShapeDtypeStruct((d_out // dt, ff, dt), _BF16)],
        scratch_shapes=[pltpu.VMEM((n_col, f32_rows, tn), _F32)],
        compiler_params=_params("arbitrary", "arbitrary"),
        name="ffn_up",
    )(h, h, ssq, ssq, w_gate, w_up, conv_w, conv_b.reshape(depth, 1, ff), w_down)


def kernel(x, positions, norm1_g, w_in, conv_a_w, conv_a_b, beta_a, ret_decay_logit, ret_gn_g, w_out,
           norm2_g, w_gate, w_up, ffn_conv_w, ffn_conv_b, w_down, norm_f_g):
    batch, seq, d_model = x.shape
    depth = w_in.shape[0]
    conv_width = conv_a_w.shape[-1]
    m = batch * seq

    xf = x.reshape(m, d_model)
    cos, sin = _rope_tables(positions)
    xg, ssq = _stream_in(xf, norm1_g[0])

    for l in range(depth):
        proj = _in_proj(xg, ssq, w_in, l)
        yr, w_o = _retention(proj, cos, sin, ret_decay_logit, ret_gn_g, w_out, l, batch, seq, 3 * conv_width)
        xf, xg, ssq = _out_proj(proj, yr, w_o, conv_a_w, conv_a_b, beta_a, l, xf, norm2_g[l], seq)
        hidden, w_dn = _ffn_up(xg, ssq, w_gate, w_up, ffn_conv_w, ffn_conv_b, w_down, l, seq)
        if l + 1 < depth:
            xf, xg, ssq = _down_proj(hidden, w_dn, xf, norm1_g[l + 1])
        else:
            xf = _down_proj(hidden, w_dn, xf)

    return _rmsnorm(xf, norm_f_g, x.dtype).reshape(batch, seq, d_model)
```

```python
import functools

import jax
import jax.numpy as jnp
from jax import lax
from jax.experimental import pallas as pl
from jax.experimental.pallas import tpu as pltpu

EPS = 1e-6
ROPE_BASE = 10000.0
HEAD_DIM = 256
CONV_GROUP = 128
CHUNK = 256
BF16_SUBLANES = 16
V7X_VMEM_LIMIT = 56 * 1024 * 1024

_BF16 = jnp.bfloat16
_F32 = jnp.float32


def _params(*sem):
    return pltpu.CompilerParams(dimension_semantics=sem, vmem_limit_bytes=V7X_VMEM_LIMIT)


def _dot(a, b):
    return jnp.dot(a, b, preferred_element_type=_F32)


def _fit(dim, tile):
    return tile if dim % tile == 0 else dim


SSQ_LANES = 128


def _row_scale(ssq, width):
    return lax.rsqrt(jnp.sum(ssq, axis=-1, keepdims=True) / width + EPS)


def _emit_norm_inputs(x_new, gain_ref, xg_ref, ssq_ref, col_step):
    xg_ref[...] = (x_new * gain_ref[...]).astype(xg_ref.dtype)
    sq = x_new * x_new
    part = sq[:, 0:SSQ_LANES]
    for c in range(1, sq.shape[1] // SSQ_LANES):
        part = part + sq[:, c * SSQ_LANES:(c + 1) * SSQ_LANES]

    @pl.when(col_step == 0)
    def _():
        ssq_ref[...] = part

    @pl.when(col_step != 0)
    def _():
        ssq_ref[...] += part


def _norm_inputs_out(m, n, tm, tn):
    specs = [pl.BlockSpec((tm, tn), lambda i, j: (i, j)), pl.BlockSpec((tm, SSQ_LANES), lambda i, j: (i, 0))]
    shapes = [jax.ShapeDtypeStruct((m, n), _BF16), jax.ShapeDtypeStruct((m, SSQ_LANES), _F32)]
    return specs, shapes


def _stream_in_body(x_ref, g_ref, xg_ref, ssq_ref):
    _emit_norm_inputs(x_ref[...], g_ref, xg_ref, ssq_ref, pl.program_id(1))


def _stream_in(x, gain, tm=256):
    m, d = x.shape
    tm = _fit(m, tm)
    specs, shapes = _norm_inputs_out(m, d, tm, d)
    return pl.pallas_call(
        _stream_in_body,
        grid=(m // tm, 1),
        in_specs=[pl.BlockSpec((tm, d), lambda i, j: (i, 0)),
                  pl.BlockSpec((1, d), lambda i, j: (0, 0))],
        out_specs=specs,
        out_shape=shapes,
        compiler_params=_params("parallel", "arbitrary"),
        name="stream_in",
    )(x, gain.reshape(1, d))


def _rmsnorm_body(x_ref, g_ref, o_ref):
    x = x_ref[...]
    ms = jnp.mean(x * x, axis=-1, keepdims=True)
    o_ref[...] = (x * lax.rsqrt(ms + EPS) * g_ref[...]).astype(o_ref.dtype)


def _rmsnorm(x, g, out_dtype, tm=256):
    m, d = x.shape
    tm = _fit(m, tm)
    return pl.pallas_call(
        _rmsnorm_body,
        grid=(m // tm,),
        in_specs=[pl.BlockSpec((tm, d), lambda i: (i, 0)),
                  pl.BlockSpec((1, d), lambda i: (0, 0))],
        out_specs=pl.BlockSpec((tm, d), lambda i: (i, 0)),
        out_shape=jax.ShapeDtypeStruct((m, d), out_dtype),
        compiler_params=_params("parallel"),
        name="rmsnorm",
    )(x, g.reshape(1, d))


def _in_proj_body(a_ref, ssq_ref, wslice_ref, o_ref, w_bf16_ref):
    j = pl.program_id(0)
    i = pl.program_id(1)
    rows = wslice_ref.shape[0]

    def round_slice():
        dst = pl.ds(pl.multiple_of(i * rows, rows), rows)
        w_bf16_ref[j % 2, dst, :] = wslice_ref[...].astype(w_bf16_ref.dtype)

    @pl.when(j == 0)
    def _():
        round_slice()

    @pl.when(j > 0)
    def _():
        scale = _row_scale(ssq_ref[...], a_ref.shape[1])
        o_ref[...] = (_dot(a_ref[...], w_bf16_ref[(j - 1) % 2]) * scale).astype(o_ref.dtype)
        round_slice()


def _in_proj(a, ssq, w, layer, tm=1024, tn=1024):
    m, k = a.shape
    n = w.shape[2]
    tm, tn = _fit(m, tm), _fit(n, tn)
    n_row, n_col = m // tm, n // tn
    rows = k // n_row
    assert rows * n_row == k and rows % BF16_SUBLANES == 0
    row_tile = lambda j, i: jnp.where(j > 0, i, 0)
    return pl.pallas_call(
        _in_proj_body,
        grid=(n_col + 1, n_row),
        in_specs=[pl.BlockSpec((tm, k), lambda j, i: (row_tile(j, i), 0)),
                  pl.BlockSpec((tm, SSQ_LANES), lambda j, i: (row_tile(j, i), 0)),
                  pl.BlockSpec((None, rows, tn), lambda j, i: (layer, i, jnp.minimum(j, n_col - 1)))],
        out_specs=pl.BlockSpec((tm, tn), lambda j, i: (row_tile(j, i), jnp.maximum(j - 1, 0))),
        out_shape=jax.ShapeDtypeStruct((m, n), _BF16),
        scratch_shapes=[pltpu.VMEM((2, k, tn), _BF16)],
        compiler_params=_params("arbitrary", "arbitrary"),
        name="in_proj",
    )(a, ssq, w)


def _prefetched_row_tile(n_col, a_hbm, buf_ref, sem_ref):
    i = pl.program_id(0)
    j = pl.program_id(1)
    tm = buf_ref.shape[1]
    chunk = tm // n_col

    def chunk_copy(tile, c):
        src = a_hbm.at[pl.ds(tile * tm + c * chunk, chunk), :]
        dst = buf_ref.at[tile % 2, pl.ds(c * chunk, chunk), :]
        return pltpu.make_async_copy(src, dst, sem_ref.at[tile % 2])

    @pl.when((i == 0) & (j == 0))
    def _():
        for c in range(n_col):
            chunk_copy(0, c).start()

    @pl.when(i + 1 < pl.num_programs(0))
    def _():
        chunk_copy(i + 1, j).start()

    @pl.when(j == 0)
    def _():
        for c in range(n_col):
            chunk_copy(i, c).wait()

    return buf_ref.at[i % 2]


def _down_proj_body(n_col, a_hbm, w_ref, x_ref, gain_ref, o_ref, xg_ref, ssq_ref, buf_ref, sem_ref):
    a_ref = _prefetched_row_tile(n_col, a_hbm, buf_ref, sem_ref)
    x_new = x_ref[...] + _dot(a_ref[...], w_ref[...])
    o_ref[...] = x_new
    _emit_norm_inputs(x_new, gain_ref, xg_ref, ssq_ref, pl.program_id(1))


def _down_proj_last_body(n_col, a_hbm, w_ref, x_ref, o_ref, buf_ref, sem_ref):
    a_ref = _prefetched_row_tile(n_col, a_hbm, buf_ref, sem_ref)
    o_ref[...] = x_ref[...] + _dot(a_ref[...], w_ref[...])


def _down_proj(a, w_tiles, x, gain=None, tm=512):
    m, k = a.shape
    n_col, _, tn = w_tiles.shape
    n = n_col * tn
    tm = _fit(m, tm)
    assert (tm // n_col) * n_col == tm and (tm // n_col) % BF16_SUBLANES == 0
    tile = pl.BlockSpec((tm, tn), lambda i, j: (i, j))
    in_specs = [pl.BlockSpec(memory_space=pl.ANY), pl.BlockSpec((None, k, tn), lambda i, j: (j, 0, 0)), tile]
    scratch = [pltpu.VMEM((2, tm, k), a.dtype), pltpu.SemaphoreType.DMA((2,))]
    params = _params("arbitrary", "arbitrary")
    if gain is None:
        return pl.pallas_call(
            functools.partial(_down_proj_last_body, n_col),
            grid=(m // tm, n_col),
            in_specs=in_specs,
            out_specs=tile,
            out_shape=jax.ShapeDtypeStruct((m, n), _F32),
            scratch_shapes=scratch,
            compiler_params=params,
            name="down_proj_last",
        )(a, w_tiles, x)
    specs, shapes = _norm_inputs_out(m, n, tm, tn)
    return pl.pallas_call(
        functools.partial(_down_proj_body, n_col),
        grid=(m // tm, n_col),
        in_specs=in_specs + [pl.BlockSpec((1, tn), lambda i, j: (0, j))],
        out_specs=[tile] + specs,
        out_shape=[jax.ShapeDtypeStruct((m, n), _F32)] + shapes,
        scratch_shapes=scratch,
        compiler_params=params,
        name="down_proj",
    )(a, w_tiles, x, gain.reshape(1, n))


PROJ_TILE = 512


def _store_weight_tiles(w_ref, tiles_ref):
    tn = tiles_ref.shape[2]
    for c in range(tiles_ref.shape[0]):
        tiles_ref[c] = w_ref[:, c * tn:(c + 1) * tn].astype(tiles_ref.dtype)


def _neighbour_rows(x, prev_row, next_row):
    tm = x.shape[0]
    rows = lax.broadcasted_iota(jnp.int32, x.shape, 0)
    up = jnp.where(rows == 0, prev_row, pltpu.roll(x, 1, 0))
    dn = jnp.where(rows == tm - 1, next_row, pltpu.roll(x, tm - 1, 0))
    return up, dn


def _conv_mixer_tile(tile_in_seq, tiles_per_seq, b_ref, c_ref, x_ref, cp_ref, xp_ref, cn_ref, xn_ref,
                     w_ref, bias_ref, beta_ref):
    f = lambda r: r[...].astype(_F32)
    u = f(c_ref) * f(x_ref)
    keep_prev = (tile_in_seq != 0).astype(_F32)
    keep_next = (tile_in_seq != tiles_per_seq - 1).astype(_F32)
    halo = cp_ref.shape[0]
    u_prev = (f(cp_ref) * f(xp_ref))[halo - 1:halo, :] * keep_prev
    u_next = (f(cn_ref) * f(xn_ref))[0:1, :] * keep_next
    up, dn = _neighbour_rows(u, u_prev, u_next)
    w = w_ref[...]
    conv = up * w[0:1, :] + u * w[1:2, :] + dn * w[2:3, :] + bias_ref[...]
    ya = f(b_ref) * conv
    beta = beta_ref[...]
    groups = []
    for g in range(ya.shape[1] // CONV_GROUP):
        sl = slice(g * CONV_GROUP, (g + 1) * CONV_GROUP)
        yg = ya[:, sl]
        ms = jnp.mean(yg * yg, axis=-1, keepdims=True)
        groups.append(yg * lax.rsqrt(ms + EPS) * beta[:, sl])
    return jnp.concatenate(groups, axis=1)


def _out_proj_body(tiles_per_seq, b_ref, c_ref, xc_ref, cp_ref, xp_ref, cn_ref, xn_ref, cw_ref, cb_ref, beta_ref,
                   yr_ref, wa_ref, wr_ref, x_ref, gain_ref, o_ref, xg_ref, ssq_ref, ya_ref):
    r = pl.program_id(0)
    j = pl.program_id(1)
    n_row = pl.num_programs(0) - 1
    cw = b_ref.shape[1]

    def mix():
        ya = _conv_mixer_tile(r % tiles_per_seq, tiles_per_seq, b_ref, c_ref, xc_ref, cp_ref, xp_ref, cn_ref, xn_ref,
                              cw_ref, cb_ref, beta_ref)
        ya_ref[r % 2, :, pl.ds(pl.multiple_of(j * cw, cw), cw)] = ya.astype(ya_ref.dtype)

    def project():
        y = _dot(ya_ref[(r - 1) % 2], wa_ref[...]) + _dot(yr_ref[...], wr_ref[...])
        x_new = x_ref[...] + y
        o_ref[...] = x_new
        _emit_norm_inputs(x_new, gain_ref, xg_ref, ssq_ref, j)

    @pl.when(r == 0)
    def _():
        mix()

    @pl.when((r > 0) & (r < n_row))
    def _():
        mix()
        project()

    @pl.when(r == n_row)
    def _():
        project()


def _out_proj(proj, yr, w_tiles, conv_w, conv_b, beta, layer, x, gain, seq, tm=1024):
    m = proj.shape[0]
    depth, _, ka = conv_w.shape
    kr = yr.shape[1]
    n_col, _, tn = w_tiles.shape
    n = n_col * tn
    assert ka == kr
    tm = _fit(seq, tm)
    n_row = m // tm
    cw = ka // n_col
    assert cw * n_col == ka and cw % CONV_GROUP == 0
    nb = ka // cw
    halo = BF16_SUBLANES
    per = tm // halo
    mix_tile = lambda r: jnp.minimum(r, n_row - 1)
    out_tile = lambda r: jnp.maximum(r - 1, 0)
    main = lambda off: pl.BlockSpec((tm, cw), lambda r, j: (mix_tile(r), off * nb + j))
    before = lambda off: pl.BlockSpec((halo, cw), lambda r, j: (jnp.maximum(mix_tile(r) * per - 1, 0), off * nb + j))
    after = lambda off: pl.BlockSpec(
        (halo, cw), lambda r, j: (jnp.minimum((mix_tile(r) + 1) * per, m // halo - 1), off * nb + j))
    vec = lambda rows: pl.BlockSpec((None, rows, cw), lambda r, j: (layer, 0, j))
    tile = pl.BlockSpec((tm, tn), lambda r, j: (out_tile(r), jnp.where(r == 0, 0, j)))
    return pl.pallas_call(
        functools.partial(_out_proj_body, seq // tm),
        grid=(n_row + 1, n_col),
        in_specs=[main(0), main(1), main(2), before(1), before(2), after(1), after(2), vec(3), vec(1), vec(1),
                  pl.BlockSpec((tm, kr), lambda r, j: (out_tile(r), 0)),
                  pl.BlockSpec((None, ka, tn), lambda r, j: (j, 0, 0)),
                  pl.BlockSpec((None, kr, tn), lambda r, j: (j, 1, 0)),
                  tile,
                  pl.BlockSpec((1, tn), lambda r, j: (0, j))],
        out_specs=[tile, tile, pl.BlockSpec((tm, SSQ_LANES), lambda r, j: (out_tile(r), 0))],
        out_shape=[jax.ShapeDtypeStruct((m, n), _F32), jax.ShapeDtypeStruct((m, n), _BF16),
                   jax.ShapeDtypeStruct((m, SSQ_LANES), _F32)],
        scratch_shapes=[pltpu.VMEM((2, tm, ka), _BF16)],
        compiler_params=_params("arbitrary", "arbitrary"),
        name="out_proj",
    )(proj, proj, proj, proj, proj, proj, proj, conv_w, conv_b.reshape(depth, 1, ka), beta.reshape(depth, 1, ka),
      yr, w_tiles, w_tiles, x, gain.reshape(1, n))


def _rope_tables_body(pos_ref, inv_ref, cos_ref, sin_ref):
    ang = pos_ref[...].astype(_F32) * inv_ref[...]
    cos_ref[...] = jnp.cos(ang)
    sin_ref[...] = jnp.sin(ang)


def _rope_tables(positions, tm=1024):
    m = positions.size
    tm = _fit(m, tm)
    half = HEAD_DIM // 2
    inv_freq = ROPE_BASE ** (-jnp.arange(half, dtype=_F32) / half)
    out = jax.ShapeDtypeStruct((m, half), _F32)
    return pl.pallas_call(
        _rope_tables_body,
        grid=(m // tm,),
        in_specs=[pl.BlockSpec((tm, 1), lambda i: (i, 0)),
                  pl.BlockSpec((1, half), lambda i: (0, 0))],
        out_specs=[pl.BlockSpec((tm, half), lambda i: (i, 0))] * 2,
        out_shape=[out, out],
        compiler_params=_params("parallel"),
        name="rope_tables",
    )(positions.reshape(m, 1), inv_freq.reshape(1, half))


def _rotary(t, cos, sin):
    half = t.shape[-1] // 2
    t1, t2 = t[:, :half], t[:, half:]
    return jnp.concatenate([t1 * cos - t2 * sin, t2 * cos + t1 * sin], axis=-1)


def _retention_body(n_heads, logit_ref, q_ref, k_ref, v_ref, g_ref, cos_ref, sin_ref, gn_ref, wo_ref,
                    o_ref, wo_tiles_ref, krot_ref, kvf_ref, kvb_ref, sf_ref, sb_ref, state_ref):
    _store_weight_tiles(wo_ref, wo_tiles_ref)

    h = pl.program_id(1)
    seq, d = q_ref.shape
    c = CHUNK
    n_chunks = seq // c

    def log_gamma(shape, direction):
        return jax.nn.log_sigmoid(jnp.full(shape, logit_ref[direction * n_heads + h], _F32))

    def chunk_rows(n):
        return pl.ds(pl.multiple_of(n * c, c), c)

    pos_cd = lax.broadcasted_iota(jnp.int32, (c, d), 0).astype(_F32)
    lf = log_gamma((c, d), 0)
    lb = log_gamma((c, d), 1)

    wk_f = jnp.exp(lf * (c - 1.0 - pos_cd))
    wk_b = jnp.exp(lb * pos_cd)

    def summarise(n, carry):
        rows = chunk_rows(n)
        k = _rotary(k_ref[rows, :].astype(_F32), cos_ref[rows, :], sin_ref[rows, :])
        krot_ref[rows, :] = k.astype(_BF16)
        v = v_ref[rows, :]
        contract_rows = (((0,), (0,)), ((), ()))
        kvf_ref[n] = lax.dot_general((k * wk_f).astype(_BF16), v, contract_rows, preferred_element_type=_F32)
        kvb_ref[n] = lax.dot_general((k * wk_b).astype(_BF16), v, contract_rows, preferred_element_type=_F32)
        return carry

    lax.fori_loop(0, n_chunks, summarise, 0, unroll=8)

    def scan(summary_ref, out_ref, log_g, order):
        g_chunk = jnp.exp(log_g * float(c))
        state_ref[...] = jnp.zeros_like(state_ref)

        def step(t, carry):
            n = order(t)
            state = state_ref[...]
            out_ref[n] = state.astype(_BF16)
            state_ref[...] = state * g_chunk + summary_ref[n]
            return carry

        lax.fori_loop(0, n_chunks, step, 0, unroll=4)

    scan(kvf_ref, sf_ref, log_gamma((1, d), 0), lambda t: t)
    scan(kvb_ref, sb_ref, log_gamma((1, d), 1), lambda t: n_chunks - 1 - t)

    row = lax.broadcasted_iota(jnp.int32, (c, c), 0)
    col = lax.broadcasted_iota(jnp.int32, (c, c), 1)
    rel = (row - col).astype(_F32)
    scale = HEAD_DIM ** -0.5
    dec = jnp.where(rel >= 0,
                    jnp.exp(log_gamma((c, c), 0) * jnp.maximum(rel, 0.0)),
                    jnp.exp(log_gamma((c, c), 1) * jnp.maximum(-rel, 0.0))) * scale
    wq_f = jnp.exp(lf * (pos_cd + 1.0)) * scale
    wq_b = jnp.exp(lb * (c - pos_cd)) * scale
    gain = gn_ref[...]

    def output(n, carry):
        rows = chunk_rows(n)
        q = _rotary(q_ref[rows, :].astype(_F32), cos_ref[rows, :], sin_ref[rows, :]).astype(_BF16)
        scores = lax.dot_general(q, krot_ref[rows, :], (((1,), (1,)), ((), ())),
                                 preferred_element_type=_F32) * dec
        o = _dot(scores.astype(_BF16), v_ref[rows, :])
        o = o + wq_f * _dot(q, sf_ref[n]) + wq_b * _dot(q, sb_ref[n])
        mu = jnp.mean(o, axis=-1, keepdims=True)
        cen = o - mu
        var = jnp.mean(cen * cen, axis=-1, keepdims=True)
        normed = cen * lax.rsqrt(var + EPS) * gain
        gate = g_ref[rows, :].astype(_F32)
        o_ref[rows, :] = (normed * jax.nn.silu(gate)).astype(o_ref.dtype)
        return carry

    lax.fori_loop(0, n_chunks, output, 0, unroll=16)


def _retention(proj, cos, sin, decay_logit, gn_g, w_out, layer, batch, seq, col0):
    m = proj.shape[0]
    depth, rw = gn_g.shape
    n_heads = rw // HEAD_DIM
    _, wo_rows, wo_cols = w_out.shape
    tn = _fit(wo_cols, PROJ_TILE)
    slab = wo_rows // (batch * n_heads)
    assert slab * batch * n_heads == wo_rows and slab % BF16_SUBLANES == 0
    first = col0 // HEAD_DIM
    n_chunks = seq // CHUNK
    part = lambda p: pl.BlockSpec((seq, HEAD_DIM), lambda b, h: (b, first + p * n_heads + h))
    table = pl.BlockSpec((seq, HEAD_DIM // 2), lambda b, h: (b, 0), pipeline_mode=pl.Buffered(1))
    summaries = pltpu.VMEM((n_chunks, HEAD_DIM, HEAD_DIM), _F32)
    states = pltpu.VMEM((n_chunks, HEAD_DIM, HEAD_DIM), _BF16)
    return pl.pallas_call(
        functools.partial(_retention_body, n_heads),
        grid=(batch, n_heads),
        in_specs=[pl.BlockSpec(memory_space=pltpu.SMEM),
                  part(0), part(1), part(2), part(3), table, table,
                  pl.BlockSpec((None, 1, HEAD_DIM), lambda b, h: (layer, 0, h)),
                  pl.BlockSpec((None, slab, wo_cols), lambda b, h: (layer, b * n_heads + h, 0))],
        out_specs=[pl.BlockSpec((seq, HEAD_DIM), lambda b, h: (b, h)),
                   pl.BlockSpec((wo_cols // tn, slab, tn), lambda b, h: (0, b * n_heads + h, 0))],
        out_shape=[jax.ShapeDtypeStruct((m, rw), _BF16),
                   jax.ShapeDtypeStruct((wo_cols // tn, wo_rows, tn), _BF16)],
        scratch_shapes=[pltpu.VMEM((seq, HEAD_DIM), _BF16), summaries, summaries, states, states,
                        pltpu.VMEM((HEAD_DIM, HEAD_DIM), _F32)],
        compiler_params=_params("parallel", "parallel"),
        name="retention",
    )(decay_logit[layer].reshape(-1), proj, proj, proj, proj, cos, sin, gn_g.reshape(depth, 1, rw), w_out)


def _ffn_up_body(tiles_per_seq, n_sub, h_ref, hn_ref, ssq_ref, ssqn_ref,
                 wg_ref, wu_ref, cw_ref, cb_ref, wd_ref, o_ref, wd_tiles_ref, carry_ref):
    i = pl.program_id(0)
    j = pl.program_id(1)

    @pl.when(i == 0)
    def _():
        _store_weight_tiles(wd_ref, wd_tiles_ref)
        carry_ref[j] = jnp.zeros(carry_ref.shape[1:], carry_ref.dtype)

    tm, d_model = h_ref.shape
    halo = hn_ref.shape[0]
    wg = wg_ref[...].astype(_BF16)
    wu = wu_ref[...].astype(_BF16)
    w = cw_ref[...]
    bias = cb_ref[...]
    t = i % tiles_per_seq
    keep = carry_ref.shape[1]
    prev_row = carry_ref[j][keep - 1:keep, :] * (t != 0).astype(_F32)
    scale_next = _row_scale(ssqn_ref[...], d_model)[0:1, :] * (t != tiles_per_seq - 1).astype(_F32)

    sub = tm // n_sub
    rows = lambda s: slice(s * sub, (s + 1) * sub)
    scale = functools.cache(lambda s: _row_scale(ssq_ref[rows(s), :], d_model))

    def gate_pre(s):
        if s < n_sub - 1:
            return _dot(h_ref[rows(s), :], wg) * scale(s), None
        g = _dot(jnp.concatenate([h_ref[rows(s), :], hn_ref[...]], axis=0), wg)
        return g[:sub, :] * scale(s), g[sub:sub + 1, :] * scale_next

    gs = {}
    for s in range(n_sub):
        gs[s], next_edge = gate_pre(s)
    carry_ref[j] = gs[n_sub - 1][sub - keep:, :]
    for s in range(n_sub):
        u = _dot(h_ref[rows(s), :], wu) * scale(s)
        above = prev_row if s == 0 else gs[s - 1][sub - 1:sub, :]
        below = next_edge if s == n_sub - 1 else gs[s + 1][0:1, :]
        g_up, g_dn = _neighbour_rows(gs[s], above, below)
        gate = g_up * w[0:1, :] + gs[s] * w[1:2, :] + g_dn * w[2:3, :] + bias
        o_ref[rows(s), :] = (jax.nn.silu(gate) * u).astype(o_ref.dtype)


def _ffn_up(h, ssq, w_gate, w_up, conv_w, conv_b, w_down, layer, seq, tm=2048, tn=256, sub=512):
    m, k = h.shape
    depth, _, ff = w_gate.shape
    d_out = w_down.shape[2]
    tm, tn = _fit(seq, tm), _fit(ff, tn)
    n_col = ff // tn
    dt = _fit(d_out, PROJ_TILE)
    halo = BF16_SUBLANES
    f32_rows = 8
    once = dict(pipeline_mode=pl.Buffered(1))
    after = lambda rows: (lambda i, j: (jnp.minimum((i + 1) * (tm // rows), m // rows - 1), 0))
    weight = pl.BlockSpec((None, k, tn), lambda i, j: (layer, 0, j))
    vec = lambda rows: pl.BlockSpec((None, rows, tn), lambda i, j: (layer, 0, j))
    return pl.pallas_call(
        functools.partial(_ffn_up_body, seq // tm, tm // _fit(tm, sub)),
        grid=(m // tm, ff // tn),
        in_specs=[pl.BlockSpec((tm, k), lambda i, j: (i, 0), **once),
                  pl.BlockSpec((halo, k), after(halo)),
                  pl.BlockSpec((tm, SSQ_LANES), lambda i, j: (i, 0), **once),
                  pl.BlockSpec((f32_rows, SSQ_LANES), after(f32_rows)),
                  weight, weight, vec(3), vec(1),
                  pl.BlockSpec((None, tn, d_out), lambda i, j: (layer, jnp.where(i == 0, j, 0), 0))],
        out_specs=[pl.BlockSpec((tm, tn), lambda i, j: (i, j)),
                   pl.BlockSpec((d_out // dt, tn, dt), lambda i, j: (0, jnp.where(i == 0, j, n_col - 1), 0))],
        out_shape=[jax.ShapeDtypeStruct((m, ff), _BF16), jax.ShapeDtypeStruct((d_out // dt, ff, dt), _BF16)],
        scratch_shapes=[pltpu.VMEM((n_col, f32_rows, tn), _F32)],
        compiler_params=_params("arbitrary", "arbitrary"),
        name="ffn_up",
    )(h, h, ssq, ssq, w_gate, w_up, conv_w, conv_b.reshape(depth, 1, ff), w_down)


def kernel(x, positions, norm1_g, w_in, conv_a_w, conv_a_b, beta_a, ret_decay_logit, ret_gn_g, w_out,
           norm2_g, w_gate, w_up, ffn_conv_w, ffn_conv_b, w_down, norm_f_g):
    batch, seq, d_model = x.shape
    depth = w_in.shape[0]
    conv_width = conv_a_w.shape[-1]
    m = batch * seq

    xf = x.reshape(m, d_model)
    cos, sin = _rope_tables(positions)
    xg, ssq = _stream_in(xf, norm1_g[0])

    for l in range(depth):
        proj = _in_proj(xg, ssq, w_in, l)
        yr, w_o = _retention(proj, cos, sin, ret_decay_logit, ret_gn_g, w_out, l, batch, seq, 3 * conv_width)
        xf, xg, ssq = _out_proj(proj, yr, w_o, conv_a_w, conv_a_b, beta_a, l, xf, norm2_g[l], seq)
        hidden, w_dn = _ffn_up(xg, ssq, w_gate, w_up, ffn_conv_w, ffn_conv_b, w_down, l, seq)
        if l + 1 < depth:
            xf, xg, ssq = _down_proj(hidden, w_dn, xf, norm1_g[l + 1])
        else:
            xf = _down_proj(hidden, w_dn, xf)

    return _rmsnorm(xf, norm_f_g, x.dtype).reshape(batch, seq, d_model)
```
